```python
import jax
import jax.numpy as jnp
from jax import lax
import numpy as np

D_MODEL = 4096
BATCH = 4
SEQ = 2048
DEPTH = 2
DEC_BATCH = 8
DEC_SEQ = 1
PAST_LEN = 16384
PAGE_SIZE = 128

HEAD_DIM = 128
N_HEADS = D_MODEL // HEAD_DIM
N_KV_HEADS = N_HEADS // 4
GROUP = N_HEADS // N_KV_HEADS
D_FF = 7 * D_MODEL // 2
N_EXPERTS = 8
MOE_TOPK = 2
MOE_ROW_BLOCK = 512
N_MIXERS = 2
MOBA_BLOCK = 256
MOBA_TOPK = 3
MOBA_Q_CHUNK = 8
FOX_Q_BLOCK = 128
ROPE_THETA = 10000.0
FORGET_BIAS = 3.0
EPS = 1e-6
ATTN_SCALE = HEAD_DIM ** -0.5

kernel_name = 'moba_fox_hybrid_decode_step'


def rms_norm(x, g):
    x32 = x.astype(jnp.float32)
    y = x32 * lax.rsqrt(jnp.mean(x32 * x32, axis=-1, keepdims=True) + EPS)
    return (y * g.astype(jnp.float32)).astype(x.dtype)


def rope(t, pos):
    half = HEAD_DIM // 2
    inv_freq = ROPE_THETA ** (-jnp.arange(half, dtype=jnp.float32) * 2.0 / HEAD_DIM)
    ang = pos.astype(jnp.float32)[:, None] * inv_freq[None, :]
    cos = jnp.cos(ang)[None, :, None, :]
    sin = jnp.sin(ang)[None, :, None, :]
    t32 = t.astype(jnp.float32)
    t1, t2 = t32[..., :half], t32[..., half:]
    return jnp.concatenate([t1 * cos - t2 * sin, t2 * cos + t1 * sin], axis=-1).astype(t.dtype)


def gather_pages(pool, page_table):
    rows = pool[page_table]
    return rows.reshape((page_table.shape[0], -1) + pool.shape[2:])


def group_heads(t):
    b, s = t.shape[:2]
    return t.reshape(b, s, N_KV_HEADS, GROUP, HEAD_DIM).transpose(0, 2, 3, 1, 4)


def ungroup_heads(o):
    b, s = o.shape[0], o.shape[3]
    return o.transpose(0, 3, 1, 2, 4).reshape(b, s, N_HEADS * HEAD_DIM)


def sweep_queries(fn, args, pos, block):
    sq = pos.shape[0]
    if sq > block and sq % block == 0:
        n = sq // block

        def split(t):
            t = t.reshape(t.shape[:3] + (n, block) + t.shape[4:])
            return jnp.moveaxis(t, 3, 0)

        out = lax.map(lambda a: fn(*a), tuple(split(t) for t in args) + (pos.reshape(n, block),))
        out = jnp.moveaxis(out, 0, 3)
        return out.reshape(out.shape[:3] + (sq,) + out.shape[5:])
    return fn(*args, pos)


def moba_attend(q, q_pos, k_all, v_all):
    b, _ = q.shape[:2]
    length = k_all.shape[1]
    nb = -(-length // MOBA_BLOCK)
    pad = nb * MOBA_BLOCK - length

    def to_blocks(t):
        t = jnp.pad(t, ((0, 0), (0, pad), (0, 0), (0, 0)))
        return t.reshape(b, nb, MOBA_BLOCK, N_KV_HEADS, HEAD_DIM).transpose(0, 3, 1, 2, 4)

    k_blk = to_blocks(k_all)
    v_blk = to_blocks(v_all)
    k_mean = jnp.mean(k_blk.astype(jnp.float32), axis=3)
    qg = group_heads(q)
    q_blk = q_pos // MOBA_BLOCK
    gate = jnp.einsum('bkgqd,bknd->bkgqn', qg.astype(jnp.float32), k_mean)
    fully_past = jnp.arange(nb, dtype=jnp.int32)[None, :] < q_blk[:, None]
    gate = jnp.where(fully_past, gate, -jnp.inf)
    _, top = lax.top_k(gate, min(MOBA_TOPK, nb))
    own = jnp.broadcast_to(q_blk[:, None], top.shape[:-1] + (1,))
    idx = jnp.concatenate([top, own], axis=-1)
    slot_ok = jnp.concatenate([top < q_blk[:, None], jnp.ones(own.shape, bool)], axis=-1)
    take = jax.vmap(jax.vmap(lambda blocks, ix: blocks[ix]))

    def attend(qc, idxc, okc, posc):
        kg = take(k_blk, idxc)
        vg = take(v_blk, idxc)
        s = jnp.einsum('bkgcd,bkgcjpd->bkgcjp', qc, kg, preferred_element_type=jnp.float32) * ATTN_SCALE
        kpos = idxc[..., None] * MOBA_BLOCK + jnp.arange(MOBA_BLOCK, dtype=jnp.int32)
        s = jnp.where(okc[..., None] & (kpos <= posc[:, None, None]), s, -jnp.inf)
        p = jax.nn.softmax(s.reshape(s.shape[:-2] + (-1,)), axis=-1).reshape(s.shape)
        return jnp.einsum('bkgcjp,bkgcjpd->bkgcd', p.astype(vg.dtype), vg)

    return ungroup_heads(sweep_queries(attend, (qg, idx, slot_ok), q_pos, MOBA_Q_CHUNK))


def fox_attend(q, q_pos, c_q, k_all, v_all, c_k):
    b, sq = q.shape[:2]
    length = k_all.shape[1]
    qg = group_heads(q)
    kt = k_all.transpose(0, 2, 1, 3)
    vt = v_all.transpose(0, 2, 1, 3)
    cq = c_q.reshape(b, sq, N_KV_HEADS, GROUP).transpose(0, 2, 3, 1)
    ck = c_k.reshape(b, length, N_KV_HEADS, GROUP).transpose(0, 2, 3, 1)
    k_pos = jnp.arange(length, dtype=jnp.int32)

    def attend(qb, cqb, posb):
        s = jnp.einsum('bkgqd,bkld->bkgql', qb, kt, preferred_element_type=jnp.float32) * ATTN_SCALE
        s = s + (cqb[..., :, None] - ck[..., None, :])
        s = jnp.where(k_pos[None, :] <= posb[:, None], s, -jnp.inf)
        p = jax.nn.softmax(s, axis=-1)
        return jnp.einsum('bkgql,bkld->bkgqd', p.astype(vt.dtype), vt)

    return ungroup_heads(sweep_queries(attend, (qg, cq), q_pos, FOX_Q_BLOCK))


def moba_mixer(h, pos, k_past, v_past, w_qkv, g_q, g_k, w_o):
    b, s, _ = h.shape
    nq, nkv = N_HEADS * HEAD_DIM, N_KV_HEADS * HEAD_DIM
    q, k, v = jnp.split(h @ w_qkv, [nq, nq + nkv], axis=-1)
    q = rope(rms_norm(q.reshape(b, s, N_HEADS, HEAD_DIM), g_q), pos)
    k = rope(rms_norm(k.reshape(b, s, N_KV_HEADS, HEAD_DIM), g_k), pos)
    v = v.reshape(b, s, N_KV_HEADS, HEAD_DIM)
    if k_past is None:
        k_all, v_all = k, v
    else:
        k_all = jnp.concatenate([k_past.astype(k.dtype), k], axis=1)
        v_all = jnp.concatenate([v_past.astype(v.dtype), v], axis=1)
    o = moba_attend(q, pos, k_all, v_all)
    return o @ w_o, k, v


def fox_mixer(h, pos, k_past, v_past, logf_past, w_qkvf, b_f, g_q, g_k, w_o):
    b, s, _ = h.shape
    nq, nkv = N_HEADS * HEAD_DIM, N_KV_HEADS * HEAD_DIM
    q, k, v, f = jnp.split(h @ w_qkvf, [nq, nq + nkv, nq + 2 * nkv], axis=-1)
    q = rms_norm(q.reshape(b, s, N_HEADS, HEAD_DIM), g_q)
    k = rms_norm(k.reshape(b, s, N_KV_HEADS, HEAD_DIM), g_k)
    v = v.reshape(b, s, N_KV_HEADS, HEAD_DIM)
    logf = jax.nn.log_sigmoid(f.astype(jnp.float32) + b_f.astype(jnp.float32))
    if k_past is None:
        k_all, v_all, logf_all = k, v, logf
    else:
        k_all = jnp.concatenate([k_past.astype(k.dtype), k], axis=1)
        v_all = jnp.concatenate([v_past.astype(v.dtype), v], axis=1)
        logf_all = jnp.concatenate([logf_past.astype(jnp.float32), logf], axis=1)
    c_all = jnp.cumsum(logf_all, axis=1)
    o = fox_attend(q, pos, c_all[:, -s:], k_all, v_all, c_all)
    return o @ w_o, k, v, logf


def swiglu(h, w_gate, w_up, w_down):
    return (jax.nn.silu(h @ w_gate) * (h @ w_up)) @ w_down


def moe_swiglu(h, w_router, w_gate, w_up, w_down):
    b, s, d = h.shape
    n_tok = b * s
    t = h.reshape(n_tok, d)
    logits = jnp.dot(t, w_router, preferred_element_type=jnp.float32)
    top_val, top_idx = lax.top_k(logits, MOE_TOPK)
    gates = jax.nn.softmax(top_val, axis=-1)
    n_asg = n_tok * MOE_TOPK
    rows = max(1, min(MOE_ROW_BLOCK, n_asg // N_EXPERTS))
    cap = -(-(n_asg + N_EXPERTS * (rows - 1)) // rows) * rows
    e_flat = top_idx.reshape(n_asg)
    tok_flat = jnp.arange(n_asg, dtype=jnp.int32) // MOE_TOPK
    order = jnp.argsort(e_flat, stable=True)
    e_sorted = e_flat[order]
    counts = jnp.zeros((N_EXPERTS,), jnp.int32).at[e_flat].add(1)
    padded = (counts + rows - 1) // rows * rows
    pad_end = jnp.cumsum(padded)
    pad_start = pad_end - padded
    grp_start = jnp.cumsum(counts) - counts
    dest = pad_start[e_sorted] + jnp.arange(n_asg, dtype=jnp.int32) - grp_start[e_sorted]
    slot_tok = jnp.full((cap,), n_tok, jnp.int32).at[dest].set(tok_flat[order])
    slot_gate = jnp.zeros((cap,), jnp.float32).at[dest].set(gates.reshape(n_asg)[order])
    blk_start = jnp.arange(0, cap, rows, dtype=jnp.int32)
    blk_expert = jnp.minimum(jnp.searchsorted(pad_end, blk_start, side='right'), N_EXPERTS - 1)
    t_pad = jnp.concatenate([t, jnp.zeros((1, d), t.dtype)], axis=0)
    x_slots = t_pad[slot_tok].reshape(cap // rows, rows, d)

    def expert_block(args):
        xb, e = args
        return swiglu(xb, w_gate[e], w_up[e], w_down[e])

    y = lax.map(expert_block, (x_slots, blk_expert)).reshape(cap, d)
    y = y * slot_gate[:, None].astype(y.dtype)
    out = jax.ops.segment_sum(y, slot_tok, num_segments=n_tok + 1)[:n_tok]
    return out.reshape(b, s, d)


def run_trunk(x, pos, past, attn_norm, ffn_norm, moba_w, fox_w, dense_w, moe_w):
    new_state = []
    for i in range(DEPTH):
        h = rms_norm(x, attn_norm[i])
        if i % N_MIXERS == 0:
            kp, vp = (None, None) if past is None else (past[0], past[1])
            a, k, v = moba_mixer(h, pos, kp, vp, *moba_w)
            new_state += [k, v]
        else:
            kp, vp, lp = (None, None, None) if past is None else (past[2], past[3], past[4])
            a, k, v, lf = fox_mixer(h, pos, kp, vp, lp, *fox_w)
            new_state += [k, v, lf]
        x = x + a
        h = rms_norm(x, ffn_norm[i])
        x = x + (swiglu(h, *dense_w) if i % 2 == 0 else moe_swiglu(h, *moe_w))
    return x, tuple(new_state)


def setup_inputs(seed: int = 0) -> dict:
    key = jax.random.key(seed)
    ks = jax.random.split(key, 28)
    f32 = jnp.float32
    n_pages = PAST_LEN // PAGE_SIZE
    n_phys = (5 * DEC_BATCH * n_pages + 3) // 4
    qkv_cols = (N_HEADS + 2 * N_KV_HEADS) * HEAD_DIM
    kv_rows = (n_phys, PAGE_SIZE, N_KV_HEADS, HEAD_DIM)
    s_in = D_MODEL ** -0.5
    s_attn = (N_HEADS * HEAD_DIM) ** -0.5
    s_ff = D_FF ** -0.5

    def normal(k, shape, scale):
        return jax.random.normal(k, shape, f32) * scale

    def gain(k, shape):
        return 1.0 + 0.02 * jax.random.normal(k, shape, f32)

    perm = jax.random.permutation(ks[7], n_phys)
    page_table = perm[: DEC_BATCH * n_pages].reshape(DEC_BATCH, n_pages).astype(jnp.int32)
    return {
        'x_prompt': normal(ks[0], (BATCH, SEQ, D_MODEL), 1.0),
        'x_sample': normal(ks[1], (DEC_BATCH, DEC_SEQ, D_MODEL), 1.0),
        'cache_k0': normal(ks[2], kv_rows, 1.0),
        'cache_v0': normal(ks[3], kv_rows, 1.0),
        'cache_k1': normal(ks[4], kv_rows, 1.0),
        'cache_v1': normal(ks[5], kv_rows, 1.0),
        'cache_logf1': jax.nn.log_sigmoid(FORGET_BIAS + normal(ks[6], (n_phys, PAGE_SIZE, N_HEADS), 1.0)),
        'page_table': page_table,
        'attn_norm': gain(ks[8], (DEPTH, D_MODEL)),
        'ffn_norm': gain(ks[9], (DEPTH, D_MODEL)),
        'w_qkv0': normal(ks[10], (D_MODEL, qkv_cols), s_in),
        'q_norm0': gain(ks[11], (HEAD_DIM,)),
        'k_norm0': gain(ks[12], (HEAD_DIM,)),
        'w_o0': normal(ks[13], (N_HEADS * HEAD_DIM, D_MODEL), s_attn),
        'w_qkvf1': normal(ks[14], (D_MODEL, qkv_cols + N_HEADS), s_in),
        'b_f1': FORGET_BIAS + normal(ks[15], (N_HEADS,), 0.5),
        'q_norm1': gain(ks[16], (HEAD_DIM,)),
        'k_norm1': gain(ks[17], (HEAD_DIM,)),
        'w_o1': normal(ks[18], (N_HEADS * HEAD_DIM, D_MODEL), s_attn),
        'w_gate0': normal(ks[19], (D_MODEL, D_FF), s_in),
        'w_up0': normal(ks[20], (D_MODEL, D_FF), s_in),
        'w_down0': normal(ks[21], (D_FF, D_MODEL), s_ff),
        'w_router1': normal(ks[22], (D_MODEL, N_EXPERTS), s_in),
        'w_gate1': normal(ks[23], (N_EXPERTS, D_MODEL, D_FF), s_in),
        'w_up1': normal(ks[24], (N_EXPERTS, D_MODEL, D_FF), s_in),
        'w_down1': normal(ks[25], (N_EXPERTS, D_FF, D_MODEL), s_ff),
    }


def reference(x_prompt, x_sample, cache_k0, cache_v0, cache_k1, cache_v1, cache_logf1, page_table,
              attn_norm, ffn_norm, w_qkv0, q_norm0, k_norm0, w_o0, w_qkvf1, b_f1, q_norm1, k_norm1, w_o1,
              w_gate0, w_up0, w_down0, w_router1, w_gate1, w_up1, w_down1):
    moba_w = (w_qkv0, q_norm0, k_norm0, w_o0)
    fox_w = (w_qkvf1, b_f1, q_norm1, k_norm1, w_o1)
    dense_w = (w_gate0, w_up0, w_down0)
    moe_w = (w_router1, w_gate1, w_up1, w_down1)
    pos_p = jnp.arange(x_prompt.shape[1], dtype=jnp.int32)
    y_prompt, (k0_p, v0_p, k1_p, v1_p, logf1_p) = run_trunk(
        x_prompt, pos_p, None, attn_norm, ffn_norm, moba_w, fox_w, dense_w, moe_w)
    past_len = page_table.shape[1] * cache_k0.shape[1]
    pos_s = past_len + jnp.arange(x_sample.shape[1], dtype=jnp.int32)
    past = (gather_pages(cache_k0, page_table), gather_pages(cache_v0, page_table),
            gather_pages(cache_k1, page_table), gather_pages(cache_v1, page_table),
            gather_pages(cache_logf1, page_table))
    y_sample, (k0_s, v0_s, k1_s, v1_s, logf1_s) = run_trunk(
        x_sample, pos_s, past, attn_norm, ffn_norm, moba_w, fox_w, dense_w, moe_w)
    return (y_prompt, y_sample, k0_p, v0_p, k1_p, v1_p, logf1_p, k0_s, v0_s, k1_s, v1_s, logf1_s)
```

```python
import functools

import jax
import jax.numpy as jnp
from jax import lax
from jax.experimental import pallas as pl
from jax.experimental.pallas import tpu as pltpu

F32 = jnp.float32
BF16 = jnp.bfloat16
I32 = jnp.int32

HEAD_DIM = 128
GROUP = 4
MOBA_BLOCK = 256
MOBA_TOPK = 3
MOE_TOPK = 2
ROPE_THETA = 10000.0
EPS = 1e-6
ATTN_SCALE = HEAD_DIM ** -0.5

LANES = 128
V7X_VMEM_BYTES = 64 * 1024 * 1024
COMPILER_SCRATCH_BYTES = 6 * 1024 * 1024
NEG_INF = float("-inf")

SAMPLE_ROWS = 128
MOE_ROW_BLOCK = 512
PAGES_PER_STEP = 8


def _cparams(n_axes, vmem_bytes):
    limit = min(int(vmem_bytes) + COMPILER_SCRATCH_BYTES, V7X_VMEM_BYTES - 2 * 1024 * 1024)
    return pltpu.CompilerParams(
        dimension_semantics=("arbitrary",) * n_axes, vmem_limit_bytes=limit)


def _rms(x, g):
    ms = jnp.mean(x * x, axis=-1, keepdims=True)
    return x * lax.rsqrt(ms + EPS) * g


def _split3(x):
    a1 = x.astype(BF16)
    r1 = x - a1.astype(F32)
    a2 = r1.astype(BF16)
    r2 = r1 - a2.astype(F32)
    return a1, a2, r2.astype(BF16)


def _dot3(parts, m):
    acc = jnp.dot(parts[0], m, preferred_element_type=F32)
    acc = acc + jnp.dot(parts[1], m, preferred_element_type=F32)
    return acc + jnp.dot(parts[2], m, preferred_element_type=F32)


def _rms_body(x_ref, g_ref, h_ref):
    h_ref[...] = _rms(x_ref[...], g_ref[...]).astype(h_ref.dtype)


def _add_rms_body(x_ref, d_ref, g_ref, xo_ref, h_ref):
    x = x_ref[...] + d_ref[...]
    xo_ref[...] = x
    h_ref[...] = _rms(x, g_ref[...]).astype(h_ref.dtype)


def _add_rms_router_body(x_ref, d_ref, g_ref, wr_ref, xo_ref, idx_ref, gate_ref, *, n_experts):
    x = x_ref[...] + d_ref[...]
    xo_ref[...] = x
    h = _rms(x, g_ref[...])
    logits = jnp.dot(h, wr_ref[...], preferred_element_type=F32, precision=lax.Precision.HIGHEST)
    lane = lax.broadcasted_iota(I32, logits.shape, 1)
    s = jnp.where(lane < n_experts, logits, NEG_INF)
    m1 = jnp.max(s, axis=1, keepdims=True)
    i1 = jnp.min(jnp.where(s == m1, lane, LANES), axis=1, keepdims=True)
    s2 = jnp.where(lane == i1, NEG_INF, s)
    m2 = jnp.max(s2, axis=1, keepdims=True)
    i2 = jnp.min(jnp.where(s2 == m2, lane, LANES), axis=1, keepdims=True)
    e = jnp.exp(m2 - m1)
    g1 = 1.0 / (1.0 + e)
    g2 = e / (1.0 + e)
    idx_ref[...] = jnp.where(lane == 0, i1, jnp.where(lane == 1, i2, 0))
    gate_ref[...] = jnp.where(lane == 0, g1, jnp.where(lane == 1, g2, 0.0))


def _row_tile(mp):
    for t in (320, 256, 128):
        if mp % t == 0:
            return t
    raise ValueError(f"unsupported row count {mp}")


def _rmsnorm(x, g):
    mp, d = x.shape
    t = _row_tile(mp)
    return pl.pallas_call(
        _rms_body,
        grid=(mp // t,),
        in_specs=[pl.BlockSpec((t, d), lambda i: (i, 0)), pl.BlockSpec((1, d), lambda i: (0, 0))],
        out_specs=pl.BlockSpec((t, d), lambda i: (i, 0)),
        out_shape=jax.ShapeDtypeStruct((mp, d), BF16),
        compiler_params=_cparams(1, 2 * t * d * 6),
        name="rmsnorm",
    )(x, g.reshape(1, d))


def _add_rmsnorm(x, delta, g):
    mp, d = x.shape
    t = _row_tile(mp)
    row = pl.BlockSpec((t, d), lambda i: (i, 0))
    return pl.pallas_call(
        _add_rms_body,
        grid=(mp // t,),
        in_specs=[row, row, pl.BlockSpec((1, d), lambda i: (0, 0))],
        out_specs=[row, row],
        out_shape=[jax.ShapeDtypeStruct((mp, d), F32), jax.ShapeDtypeStruct((mp, d), BF16)],
        compiler_params=_cparams(1, 2 * t * d * 14),
        name="add_rmsnorm",
    )(x, delta, g.reshape(1, d))


def _add_rmsnorm_router(x, delta, g, w_router):
    mp, d = x.shape
    n_experts = w_router.shape[1]
    t = _row_tile(mp)
    wr = jnp.pad(w_router, ((0, 0), (0, LANES - n_experts)))
    row = pl.BlockSpec((t, d), lambda i: (i, 0))
    small = pl.BlockSpec((t, LANES), lambda i: (i, 0))
    return pl.pallas_call(
        functools.partial(_add_rms_router_body, n_experts=n_experts),
        grid=(mp // t,),
        in_specs=[row, row, pl.BlockSpec((1, d), lambda i: (0, 0)), pl.BlockSpec((d, LANES), lambda i: (0, 0))],
        out_specs=[row, small, small],
        out_shape=[jax.ShapeDtypeStruct((mp, d), F32), jax.ShapeDtypeStruct((mp, LANES), I32),
                   jax.ShapeDtypeStruct((mp, LANES), F32)],
        compiler_params=_cparams(1, 2 * t * d * 12 + 2 * d * LANES * 4 + 8 * t * d),
        name="add_rmsnorm_router",
    )(x, delta, g.reshape(1, d), wr)


def _mm_body(x_ref, w_ref, o_ref, wbf_ref):
    wbf_ref[...] = w_ref[...].astype(BF16)
    o_ref[...] = jnp.dot(x_ref[...], wbf_ref[...], preferred_element_type=F32).astype(o_ref.dtype)


def _matmul(x, w, n_out, bn, col_block0=0, bm=None):
    mp, k = x.shape
    bm = bm or mp // 8
    assert mp % bm == 0 and n_out % bn == 0
    vmem = 2 * (bm * k * 2 + k * bn * 4 + bm * bn * 4) + k * bn * 2
    return pl.pallas_call(
        _mm_body,
        grid=(mp // bm, n_out // bn),
        in_specs=[pl.BlockSpec((bm, k), lambda m, n: (m, 0)),
                  pl.BlockSpec((k, bn), lambda m, n: (0, n + col_block0))],
        out_specs=pl.BlockSpec((bm, bn), lambda m, n: (m, n)),
        out_shape=jax.ShapeDtypeStruct((mp, n_out), F32),
        scratch_shapes=[pltpu.VMEM((k, bn), BF16)],
        compiler_params=_cparams(2, vmem),
        name="matmul",
    )(x, w)


def _gu_body(be_ref, nu_ref, x_ref, wg_ref, wu_ref, o_ref, wgb_ref, wub_ref):
    b = pl.program_id(1)
    valid = b < nu_ref[0]
    prev = be_ref[jnp.maximum(b - 1, 0)]
    fresh = jnp.logical_or(b == 0, be_ref[b] != prev)

    @pl.when(jnp.logical_and(valid, fresh))
    def _():
        wgb_ref[...] = wg_ref[...].astype(BF16)
        wub_ref[...] = wu_ref[...].astype(BF16)

    @pl.when(valid)
    def _():
        x = x_ref[...]
        g = jnp.dot(x, wgb_ref[...], preferred_element_type=F32)
        u = jnp.dot(x, wub_ref[...], preferred_element_type=F32)
        o_ref[...] = (g * jax.nn.sigmoid(g) * u).astype(o_ref.dtype)


def _ffn_gate_up(x, w_gate, w_up, blk_expert, n_used, bm, bn):
    rows, k = x.shape
    _, _, f = w_gate.shape
    nblk = rows // bm
    assert rows % bm == 0 and f % bn == 0

    def xmap(n, b, be, nu):
        return (jnp.minimum(b, nu[0] - 1), 0)

    def wmap(n, b, be, nu):
        return (be[jnp.minimum(b, nu[0] - 1)], 0, n)

    def omap(n, b, be, nu):
        return (jnp.minimum(b, nu[0] - 1), n)

    vmem = 2 * (bm * k * 2 + 2 * k * bn * 4 + bm * bn * 2) + 2 * k * bn * 2 + 3 * bm * bn * 4
    return pl.pallas_call(
        _gu_body,
        grid_spec=pltpu.PrefetchScalarGridSpec(
            num_scalar_prefetch=2,
            grid=(f // bn, nblk),
            in_specs=[pl.BlockSpec((bm, k), xmap),
                      pl.BlockSpec((None, k, bn), wmap),
                      pl.BlockSpec((None, k, bn), wmap)],
            out_specs=pl.BlockSpec((bm, bn), omap),
            scratch_shapes=[pltpu.VMEM((k, bn), BF16), pltpu.VMEM((k, bn), BF16)]),
        out_shape=jax.ShapeDtypeStruct((rows, f), BF16),
        compiler_params=_cparams(2, vmem),
        name="ffn_gate_up",
    )(blk_expert, n_used, x, w_gate, w_up)


def _dn_body(be_ref, nu_ref, a_ref, w_ref, s_ref, o_ref, wb_ref):
    b = pl.program_id(0)
    k = pl.program_id(2)
    valid = b < nu_ref[0]

    @pl.when(valid)
    def _():
        wb_ref[...] = w_ref[...].astype(BF16)
        part = jnp.dot(a_ref[...], wb_ref[...], preferred_element_type=F32)

        @pl.when(k == 0)
        def _():
            o_ref[...] = part

        @pl.when(k > 0)
        def _():
            o_ref[...] += part

        @pl.when(k == pl.num_programs(2) - 1)
        def _():
            o_ref[...] = o_ref[...] * s_ref[...]


def _ffn_down(act, w_down, row_scale, blk_expert, n_used, bm, bn, bk):
    rows, f = act.shape
    _, _, d = w_down.shape
    nblk = rows // bm
    assert rows % bm == 0 and d % bn == 0 and f % bk == 0

    def eff(b, n, k, nu):
        skip = b >= nu[0]
        return (jnp.minimum(b, nu[0] - 1), jnp.where(skip, d // bn - 1, n), jnp.where(skip, f // bk - 1, k))

    def amap(b, n, k, be, nu):
        b, n, k = eff(b, n, k, nu)
        return (b, k)

    def wmap(b, n, k, be, nu):
        b, n, k = eff(b, n, k, nu)
        return (be[b], k, n)

    def smap(b, n, k, be, nu):
        return (jnp.minimum(b, nu[0] - 1), 0)

    def omap(b, n, k, be, nu):
        b, n, k = eff(b, n, k, nu)
        return (b, n)

    vmem = 2 * (bm * bk * 2 + bk * bn * 4 + bm * bn * 4 + bm * LANES * 4) + bk * bn * 2 + bm * bn * 4
    return pl.pallas_call(
        _dn_body,
        grid_spec=pltpu.PrefetchScalarGridSpec(
            num_scalar_prefetch=2,
            grid=(nblk, d // bn, f // bk),
            in_specs=[pl.BlockSpec((bm, bk), amap),
                      pl.BlockSpec((None, bk, bn), wmap),
                      pl.BlockSpec((bm, 1), smap)],
            out_specs=pl.BlockSpec((bm, bn), omap),
            scratch_shapes=[pltpu.VMEM((bk, bn), BF16)]),
        out_shape=jax.ShapeDtypeStruct((rows, d), F32),
        compiler_params=_cparams(3, vmem),
        name="ffn_down",
    )(blk_expert, n_used, act, w_down, row_scale)


def _qk_post_body(*refs, n_q, n_kv, rope, forget):
    it = iter(refs)
    qkv_ref = next(it)
    cos_ref = sin_ref = f_ref = bf_ref = lf_ref = None
    if rope:
        cos_ref, sin_ref = next(it), next(it)
    gq_ref, gk_ref = next(it), next(it)
    if forget:
        f_ref, bf_ref = next(it), next(it)
    q_ref, k_ref, v_ref = next(it), next(it), next(it)
    if forget:
        lf_ref = next(it)
    hd = HEAD_DIM
    for h in range(n_q + n_kv):
        t = qkv_ref[:, h * hd:(h + 1) * hd]
        y = _rms(t, gq_ref[...] if h < n_q else gk_ref[...])
        if rope:
            y = y * cos_ref[...] + pltpu.roll(y, hd // 2, 1) * sin_ref[...]
        if h < n_q:
            q_ref[:, h * hd:(h + 1) * hd] = y.astype(q_ref.dtype)
        else:
            k_ref[:, (h - n_q) * hd:(h - n_q + 1) * hd] = y
    v_ref[...] = qkv_ref[:, (n_q + n_kv) * hd:(n_q + 2 * n_kv) * hd]
    if forget:
        z = f_ref[...] + bf_ref[...]
        lf_ref[...] = jnp.minimum(z, 0.0) - jnp.log(1.0 + jnp.exp(-jnp.abs(z)))


def _qk_post(qkv, gq, gk, n_q, n_kv, cos=None, sin=None, fcols=None, b_f=None):
    mp = qkv.shape[0]
    hd = HEAD_DIM
    t = _row_tile(mp)
    rope = cos is not None
    forget = fcols is not None
    row = lambda w: pl.BlockSpec((t, w), lambda i: (i, 0))
    const = lambda w: pl.BlockSpec((1, w), lambda i: (0, 0))
    args, specs = [qkv], [row(qkv.shape[1])]
    if rope:
        args += [cos, sin]
        specs += [row(hd), row(hd)]
    args += [gq.reshape(1, hd), gk.reshape(1, hd)]
    specs += [const(hd), const(hd)]
    if forget:
        args += [fcols, b_f]
        specs += [row(LANES), const(LANES)]
    out_shape = [jax.ShapeDtypeStruct((mp, n_q * hd), BF16), jax.ShapeDtypeStruct((mp, n_kv * hd), F32),
                 jax.ShapeDtypeStruct((mp, n_kv * hd), F32)]
    out_specs = [row(n_q * hd), row(n_kv * hd), row(n_kv * hd)]
    if forget:
        out_shape.append(jax.ShapeDtypeStruct((mp, LANES), F32))
        out_specs.append(row(LANES))
    vmem = 2 * t * (qkv.shape[1] * 4 + n_q * hd * 2 + 2 * n_kv * hd * 4 + 4 * LANES * 4)
    return pl.pallas_call(
        functools.partial(_qk_post_body, n_q=n_q, n_kv=n_kv, rope=rope, forget=forget),
        grid=(mp // t,),
        in_specs=specs,
        out_specs=out_specs,
        out_shape=out_shape,
        compiler_params=_cparams(1, vmem),
        name="qk_post",
    )(*args)


def _attn_prompt_body(*refs, mode, seq, tq):
    if mode == "fox":
        q_ref, k_ref, v_ref, cq_ref, ck_ref, o_ref = refs
    else:
        q_ref, k_ref, v_ref, o_ref = refs
    hd = HEAD_DIM
    nblk = seq // tq
    kf = k_ref[...]
    kb = kf.astype(BF16)
    vb = v_ref[...].astype(BF16)
    row = lax.broadcasted_iota(I32, (tq, tq), 0)
    col = lax.broadcasted_iota(I32, (tq, tq), 1)
    causal = col <= row
    if mode == "moba":
        kmean = jnp.mean(kf.reshape(nblk, tq, hd), axis=1)
        kmean = jnp.concatenate([kmean, jnp.zeros((LANES - nblk, hd), F32)], axis=0)
        lane = lax.broadcasted_iota(I32, (tq, LANES), 1)
        eb = lax.broadcasted_iota(I32, (LANES, seq), 0)
        ec = lax.broadcasted_iota(I32, (LANES, seq), 1)
        expand = jnp.where(jnp.logical_and(ec >= eb * tq, ec < (eb + 1) * tq), 1.0, 0.0).astype(BF16)
    for i in range(nblk):
        n_past = i * tq
        n = n_past + tq
        for hh in range(GROUP):
            qh = q_ref[i * tq:(i + 1) * tq, hh * hd:(hh + 1) * hd]
            s = lax.dot_general(qh, kb[:n], (((1,), (1,)), ((), ())), preferred_element_type=F32) * ATTN_SCALE
            if mode == "fox":
                s = s + (cq_ref[i * tq:(i + 1) * tq, hh:hh + 1] - ck_ref[hh:hh + 1, :n])
            s_diag = jnp.where(causal, s[:, n_past:], NEG_INF)
            if i == 0:
                s = s_diag
            else:
                s_past = s[:, :n_past]
                if mode == "moba" and i > MOBA_TOPK:
                    gate = lax.dot_general(qh.astype(F32), kmean, (((1,), (1,)), ((), ())),
                                           preferred_element_type=F32, precision=lax.Precision.HIGHEST)
                    gm = jnp.where(lane < i, gate, NEG_INF)
                    rank = jnp.zeros((tq, LANES), I32)
                    for d in range(1, i):
                        lower = pltpu.roll(gm, d, 1)
                        higher = pltpu.roll(gm, LANES - d, 1)
                        rank = rank + (lower >= gm).astype(I32) + (higher > gm).astype(I32)
                    sel = jnp.logical_and(rank < MOBA_TOPK, lane < i)
                    picked = jnp.dot(jnp.where(sel, 1.0, 0.0).astype(BF16), expand[:, :n_past],
                                     preferred_element_type=F32)
                    s_past = jnp.where(picked > 0.5, s_past, NEG_INF)
                s = jnp.concatenate([s_past, s_diag], axis=1)
            m = jnp.max(s, axis=1, keepdims=True)
            p = jnp.exp(s - m)
            l = jnp.sum(p, axis=1, keepdims=True)
            o = jnp.dot(p.astype(BF16), vb[:n], preferred_element_type=F32) / l
            o_ref[i * tq:(i + 1) * tq, hh * hd:(hh + 1) * hd] = o.astype(o_ref.dtype)


def _attn_prompt(q, k, v, batch, seq, mode, cq=None, ck=None):
    mp = q.shape[0]
    hd = HEAD_DIM
    n_kv = k.shape[1] // hd
    tq = MOBA_BLOCK
    in_specs = [pl.BlockSpec((seq, GROUP * hd), lambda b, g: (b, g)),
                pl.BlockSpec((seq, hd), lambda b, g: (b, g)),
                pl.BlockSpec((seq, hd), lambda b, g: (b, g))]
    args = [q, k, v]
    if mode == "fox":
        in_specs += [pl.BlockSpec((None, None, seq, GROUP), lambda b, g: (b, g, 0, 0)),
                     pl.BlockSpec((None, None, GROUP, seq), lambda b, g: (b, g, 0, 0))]
        args += [cq, ck]
    vmem = 2 * (2 * seq * GROUP * hd * 2 + 2 * seq * hd * 4 + seq * LANES * 4 + 8 * seq * 4) + 20 * tq * seq * 4
    return pl.pallas_call(
        functools.partial(_attn_prompt_body, mode=mode, seq=seq, tq=tq),
        grid=(batch, n_kv),
        in_specs=in_specs,
        out_specs=pl.BlockSpec((seq, GROUP * hd), lambda b, g: (b, g)),
        out_shape=jax.ShapeDtypeStruct((mp, q.shape[1]), BF16),
        compiler_params=_cparams(2, vmem),
        name=f"attn_prompt_{mode}",
    )(*args)


def _cumsum_body(x_ref, o_ref):
    rows, seq = x_ref.shape
    r = lax.broadcasted_iota(I32, (LANES, LANES), 0)
    c = lax.broadcasted_iota(I32, (LANES, LANES), 1)
    tri = jnp.where(r <= c, 1.0, 0.0).astype(BF16)
    ones = jnp.ones((LANES, LANES), BF16)
    carry = jnp.zeros((rows, LANES), F32)
    for ch in range(seq // LANES):
        parts = _split3(x_ref[:, ch * LANES:(ch + 1) * LANES])
        o_ref[:, ch * LANES:(ch + 1) * LANES] = carry + _dot3(parts, tri)
        carry = carry + _dot3(parts, ones)


def _cumsum_lanes(x):
    b, r, l = x.shape
    spec = pl.BlockSpec((None, r, l), lambda i: (i, 0, 0))
    return pl.pallas_call(
        _cumsum_body, grid=(b,), in_specs=[spec], out_specs=spec,
        out_shape=jax.ShapeDtypeStruct(x.shape, F32),
        compiler_params=_cparams(1, 4 * r * l * 4),
        name="cumsum_lanes",
    )(x)


def _kmean_body(pt_ref, *refs, pages, per_block):
    page_refs, o_ref = refs[:pages], refs[pages]
    for j in range(pages // per_block):
        tot = jnp.sum(page_refs[j * per_block][...], axis=0, keepdims=True)
        for t in range(1, per_block):
            tot = tot + jnp.sum(page_refs[j * per_block + t][...], axis=0, keepdims=True)
        o_ref[j:j + 1, :] = tot * (1.0 / (per_block * page_refs[0].shape[0]))


def _cache_block_means(cache2d, page_table, page):
    db, n_pages = page_table.shape
    width = cache2d.shape[1]
    per_block = MOBA_BLOCK // page
    pages = PAGES_PER_STEP
    assert n_pages % pages == 0 and pages % per_block == 0
    blocks = pages // per_block

    def pmap(j):
        return lambda b, c, pt: (pt[b, c * pages + j], 0)

    out = pl.pallas_call(
        functools.partial(_kmean_body, pages=pages, per_block=per_block),
        grid_spec=pltpu.PrefetchScalarGridSpec(
            num_scalar_prefetch=1,
            grid=(db, n_pages // pages),
            in_specs=[pl.BlockSpec((page, width), pmap(j)) for j in range(pages)],
            out_specs=pl.BlockSpec((None, None, blocks, width), lambda b, c, pt: (b, c, 0, 0))),
        out_shape=jax.ShapeDtypeStruct((db, n_pages // pages, blocks, width), F32),
        compiler_params=_cparams(2, 2 * pages * page * width * 4 + 2 * 8 * width * 4),
        name="cache_block_means",
    )(page_table, *([cache2d] * pages))
    return out.reshape(db, n_pages // per_block, width)


def _moba_select_body(q_ref, km_ref, o_ref, *, n_kv):
    hd = HEAD_DIM
    q = q_ref[...]
    n_heads = q.shape[0]
    nb = km_ref.shape[0]
    rowg = lax.broadcasted_iota(I32, (n_heads, nb), 0) // GROUP
    gate = jnp.zeros((n_heads, nb), F32)
    for g in range(n_kv):
        gg = lax.dot_general(q, km_ref[:, g * hd:(g + 1) * hd], (((1,), (1,)), ((), ())),
                             preferred_element_type=F32, precision=lax.Precision.HIGHEST)
        gate = jnp.where(rowg == g, gg, gate)
    lane = lax.broadcasted_iota(I32, (n_heads, nb), 1)
    olane = lax.broadcasted_iota(I32, (n_heads, LANES), 1)
    out = jnp.zeros((n_heads, LANES), I32)
    for j in range(MOBA_TOPK):
        m = jnp.max(gate, axis=1, keepdims=True)
        idx = jnp.min(jnp.where(gate == m, lane, nb), axis=1, keepdims=True)
        out = jnp.where(olane == j, idx, out)
        gate = jnp.where(lane == idx, NEG_INF, gate)
    o_ref[...] = out


def _moba_select(q_s, kmean, n_kv):
    db, n_heads, hd = q_s.shape
    nb, width = kmean.shape[1:]
    out = pl.pallas_call(
        functools.partial(_moba_select_body, n_kv=n_kv),
        grid=(db,),
        in_specs=[pl.BlockSpec((None, n_heads, hd), lambda b: (b, 0, 0)),
                  pl.BlockSpec((None, nb, width), lambda b: (b, 0, 0))],
        out_specs=pl.BlockSpec((None, n_heads, LANES), lambda b: (b, 0, 0)),
        out_shape=jax.ShapeDtypeStruct((db, n_heads, LANES), I32),
        compiler_params=_cparams(1, 4 * nb * width * 4),
        name="moba_select",
    )(q_s, kmean)
    return out[:, :, :MOBA_TOPK]


def _moba_sample_body(pt_ref, sel_ref, q_ref, kn_ref, vn_ref, *refs, n_pages):
    k_refs, v_refs, o_ref = refs[:n_pages], refs[n_pages:2 * n_pages], refs[2 * n_pages]
    q = q_ref[...]
    q8 = jnp.broadcast_to(q, (8, HEAD_DIM)).astype(BF16)
    scores = [lax.dot_general(q8, kr[...].astype(BF16), (((1,), (1,)), ((), ())),
                              preferred_element_type=F32)[0:1] * ATTN_SCALE for kr in k_refs]
    s_own = jnp.sum(q * kn_ref[...], axis=1, keepdims=True) * ATTN_SCALE
    m = s_own
    for s in scores:
        m = jnp.maximum(m, jnp.max(s, axis=1, keepdims=True))
    p_own = jnp.exp(s_own - m)
    l = p_own
    acc = p_own * vn_ref[...]
    for s, vr in zip(scores, v_refs):
        p = jnp.exp(s - m)
        l = l + jnp.sum(p, axis=1, keepdims=True)
        p8 = jnp.broadcast_to(p, (8, p.shape[1])).astype(BF16)
        acc = acc + jnp.dot(p8, vr[...].astype(BF16), preferred_element_type=F32)[0:1]
    o_ref[...] = acc / l


def _moba_sample_attend(q_s, k_new, v_new, ck2d, cv2d, page_table, sel, page):
    db, n_heads, hd = q_s.shape
    per_block = MOBA_BLOCK // page
    n_pages = MOBA_TOPK * per_block
    n_tab = page_table.shape[1]

    def pmap(j):
        blk, t = j // per_block, j % per_block
        return lambda b, h, pt, sl: (pt[b, sl[b, h * MOBA_TOPK + blk] * per_block + t], h // GROUP)

    one = lambda b, h, pt, sl: (b, h, 0, 0)
    kv_one = lambda b, h, pt, sl: (b, h // GROUP, 0, 0)
    row = (None, None, 1, hd)
    page_spec = [pl.BlockSpec((page, hd), pmap(j)) for j in range(n_pages)]
    out = pl.pallas_call(
        functools.partial(_moba_sample_body, n_pages=n_pages),
        grid_spec=pltpu.PrefetchScalarGridSpec(
            num_scalar_prefetch=2,
            grid=(db, n_heads),
            in_specs=[pl.BlockSpec(row, one), pl.BlockSpec(row, kv_one), pl.BlockSpec(row, kv_one)]
            + page_spec + page_spec,
            out_specs=pl.BlockSpec(row, one)),
        out_shape=jax.ShapeDtypeStruct((db, n_heads, 1, hd), F32),
        compiler_params=_cparams(2, 2 * 2 * n_pages * page * hd * 4 + 64 * 1024),
        name="moba_sample_attend",
    )(page_table, sel.reshape(db, n_heads * MOBA_TOPK), q_s.reshape(db, n_heads, 1, hd),
      k_new.reshape(db, -1, 1, hd), v_new.reshape(db, -1, 1, hd), *([ck2d] * n_pages), *([cv2d] * n_pages))
    del n_tab
    return out.reshape(db, n_heads * hd)


def _page_gather_body(pt_ref, *refs, pages):
    o_ref = refs[pages]
    rows = refs[0].shape[0]
    for j in range(pages):
        o_ref[j * rows:(j + 1) * rows, :] = refs[j][...]


def _gather_logf_pages(cache_logf, page_table):
    db, n_tab = page_table.shape
    _, page, n_heads = cache_logf.shape
    pages = 2 * PAGES_PER_STEP
    assert n_tab % pages == 0

    def pmap(j):
        return lambda b, c, pt: (pt[b, c * pages + j], 0, 0)

    return pl.pallas_call(
        functools.partial(_page_gather_body, pages=pages),
        grid_spec=pltpu.PrefetchScalarGridSpec(
            num_scalar_prefetch=1,
            grid=(db, n_tab // pages),
            in_specs=[pl.BlockSpec((None, page, n_heads), pmap(j)) for j in range(pages)],
            out_specs=pl.BlockSpec((None, pages * page, n_heads), lambda b, c, pt: (b, c, 0))),
        out_shape=jax.ShapeDtypeStruct((db, n_tab * page, n_heads), F32),
        compiler_params=_cparams(2, 4 * pages * page * LANES * 4),
        name="gather_logf_pages",
    )(page_table, *([cache_logf] * pages))


def _fox_sample_body(pt_ref, qbd_ref, lfn_ref, kn_ref, vn_ref, lf_ref, *refs, pages):
    k_refs, v_refs = refs[:pages], refs[pages:2 * pages]
    o_ref, m_ref, l_ref, acc_ref, r_ref = refs[2 * pages:]
    c = pl.program_id(1)
    hd = HEAD_DIM
    n_heads = qbd_ref.shape[0]
    page = k_refs[0].shape[0]

    @pl.when(c == 0)
    def _():
        m_ref[...] = jnp.full(m_ref.shape, NEG_INF, F32)
        l_ref[...] = jnp.zeros(l_ref.shape, F32)
        acc_ref[...] = jnp.zeros(acc_ref.shape, F32)
        r_ref[...] = lfn_ref[...]

    rr = lax.broadcasted_iota(I32, (page, page), 0)
    cc = lax.broadcasted_iota(I32, (page, page), 1)
    later = jnp.where(rr > cc, 1.0, 0.0).astype(BF16)
    ones = jnp.ones((page, page), BF16)
    qbd = qbd_ref[...]
    for j in range(pages - 1, -1, -1):
        parts = _split3(lf_ref[:, j * page:(j + 1) * page])
        r = r_ref[...]
        bias = r + _dot3(parts, later)
        r_ref[...] = r + _dot3(parts, ones)
        s = lax.dot_general(qbd, k_refs[j][...].astype(BF16), (((1,), (1,)), ((), ())),
                            preferred_element_type=F32) * ATTN_SCALE + bias
        m_old = m_ref[...]
        m_new = jnp.maximum(m_old, jnp.max(s, axis=1, keepdims=True))
        alpha = jnp.exp(m_old - m_new)
        p = jnp.exp(s - m_new[:, 0:1])
        l_ref[...] = alpha * l_ref[...] + jnp.sum(p, axis=1, keepdims=True)
        acc_ref[...] = acc_ref[...] * alpha[:, 0:1] + jnp.dot(
            p.astype(BF16), v_refs[j][...].astype(BF16), preferred_element_type=F32)
        m_ref[...] = m_new

    @pl.when(c == pl.num_programs(1) - 1)
    def _():
        s_own = jnp.sum(qbd.astype(F32) * kn_ref[...], axis=1, keepdims=True) * ATTN_SCALE
        m_old = m_ref[...]
        m_new = jnp.maximum(m_old, s_own)
        alpha = jnp.exp(m_old - m_new)
        p_own = jnp.exp(s_own - m_new[:, 0:1])
        l = alpha * l_ref[...] + p_own
        full = (acc_ref[...] * alpha[:, 0:1] + p_own * vn_ref[...]) / l[:, 0:1]
        rowg = lax.broadcasted_iota(I32, (n_heads, hd), 0) // GROUP
        out = jnp.zeros((n_heads, hd), F32)
        for g in range(full.shape[1] // hd):
            out = jnp.where(rowg == g, full[:, g * hd:(g + 1) * hd], out)
        o_ref[...] = out


def _fox_sample_attend(qbd, lf_new, k_new, v_new, lf_t, ck2d, cv2d, page_table, page):
    db, n_heads, width = qbd.shape
    n_tab = page_table.shape[1]
    pages = PAGES_PER_STEP
    n_chunks = n_tab // pages
    assert n_tab % pages == 0

    def pmap(j):
        return lambda b, c, pt: (pt[b, (n_chunks - 1 - c) * pages + j], 0)

    per_b = lambda b, c, pt: (b, 0, 0)
    page_spec = [pl.BlockSpec((page, width), pmap(j)) for j in range(pages)]
    out = pl.pallas_call(
        functools.partial(_fox_sample_body, pages=pages),
        grid_spec=pltpu.PrefetchScalarGridSpec(
            num_scalar_prefetch=1,
            grid=(db, n_chunks),
            in_specs=[pl.BlockSpec((None, n_heads, width), per_b),
                      pl.BlockSpec((None, n_heads, LANES), per_b),
                      pl.BlockSpec((None, 1, width), per_b),
                      pl.BlockSpec((None, 1, width), per_b),
                      pl.BlockSpec((None, n_heads, pages * page), lambda b, c, pt: (b, 0, n_chunks - 1 - c))]
            + page_spec + page_spec,
            out_specs=pl.BlockSpec((None, n_heads, HEAD_DIM), per_b),
            scratch_shapes=[pltpu.VMEM((n_heads, LANES), F32), pltpu.VMEM((n_heads, LANES), F32),
                            pltpu.VMEM((n_heads, width), F32), pltpu.VMEM((n_heads, LANES), F32)]),
        out_shape=jax.ShapeDtypeStruct((db, n_heads, HEAD_DIM), F32),
        compiler_params=_cparams(2, 2 * 2 * pages * page * width * 4 + 4 * 1024 * 1024),
        name="fox_sample_attend",
    )(page_table, qbd, lf_new, k_new, v_new, lf_t, *([ck2d] * pages), *([cv2d] * pages))
    return out.reshape(db, n_heads * HEAD_DIM)


def _row_copy(src_hbm, row, dst, slot, sem):
    return pltpu.make_async_copy(src_hbm.at[pl.ds(row, 1)], dst.at[pl.ds(slot, 1)], sem)


def _gather_norm_body(tok_ref, nrows_ref, x_hbm, g_ref, o_ref, buf_ref, sem, *, rows):
    i = pl.program_id(0)
    base = i * rows

    @pl.when(base < nrows_ref[0])
    def _():
        def issue(r, carry):
            _row_copy(x_hbm, tok_ref[base + r], buf_ref, r, sem).start()
            return carry

        def wait(r, carry):
            _row_copy(x_hbm, tok_ref[base + r], buf_ref, r, sem).wait()
            return carry

        lax.fori_loop(0, rows, issue, 0)
        lax.fori_loop(0, rows, wait, 0)
        o_ref[...] = _rms(buf_ref[...], g_ref[...]).astype(o_ref.dtype)

    @pl.when(base >= nrows_ref[0])
    def _():
        o_ref[...] = jnp.zeros(o_ref.shape, o_ref.dtype)


def _moe_gather_norm(x, g, slot_tok, n_rows_used, rows=256):
    cap = slot_tok.shape[0]
    d = x.shape[1]
    assert cap % rows == 0
    return pl.pallas_call(
        functools.partial(_gather_norm_body, rows=rows),
        grid_spec=pltpu.PrefetchScalarGridSpec(
            num_scalar_prefetch=2,
            grid=(cap // rows,),
            in_specs=[pl.BlockSpec(memory_space=pl.ANY), pl.BlockSpec((1, d), lambda i, t, n: (0, 0))],
            out_specs=pl.BlockSpec((rows, d), lambda i, t, n: (i, 0)),
            scratch_shapes=[pltpu.VMEM((rows, d), F32), pltpu.SemaphoreType.DMA(())]),
        out_shape=jax.ShapeDtypeStruct((cap, d), BF16),
        compiler_params=_cparams(1, rows * d * 4 * 3 + 2 * rows * d * 2),
        name="moe_gather_norm",
    )(slot_tok, n_rows_used, x, g.reshape(1, d))


def _combine_body(pos_ref, x_ref, y_hbm, o_ref, buf_ref, sem, *, rows, n_tok):
    i = pl.program_id(0)
    base = i * rows

    def issue(r, carry):
        for j in range(MOE_TOPK):
            _row_copy(y_hbm, pos_ref[(base + r) * MOE_TOPK + j], buf_ref.at[j], r, sem).start()
        return carry

    def wait(r, carry):
        for j in range(MOE_TOPK):
            _row_copy(y_hbm, pos_ref[(base + r) * MOE_TOPK + j], buf_ref.at[j], r, sem).wait()
        return carry

    lax.fori_loop(0, rows, issue, 0)
    lax.fori_loop(0, rows, wait, 0)
    tok = base + lax.broadcasted_iota(I32, (rows, 1), 0)
    y = buf_ref[0]
    for j in range(1, MOE_TOPK):
        y = y + buf_ref[j]
    o_ref[...] = x_ref[...] + jnp.where(tok < n_tok, y, 0.0)


def _moe_combine(x, y, pos, n_tok, rows=128):
    mp, d = x.shape
    assert mp % rows == 0
    return pl.pallas_call(
        functools.partial(_combine_body, rows=rows, n_tok=n_tok),
        grid_spec=pltpu.PrefetchScalarGridSpec(
            num_scalar_prefetch=1,
            grid=(mp // rows,),
            in_specs=[pl.BlockSpec((rows, d), lambda i, p: (i, 0)), pl.BlockSpec(memory_space=pl.ANY)],
            out_specs=pl.BlockSpec((rows, d), lambda i, p: (i, 0)),
            scratch_shapes=[pltpu.VMEM((MOE_TOPK, rows, d), F32), pltpu.SemaphoreType.DMA(())]),
        out_shape=jax.ShapeDtypeStruct((mp, d), F32),
        compiler_params=_cparams(1, (MOE_TOPK + 4) * rows * d * 4),
        name="moe_combine",
    )(pos, x, y)


def _moe_plan(top_idx, top_gate, n_experts, zero_row):
    n_tok = top_idx.shape[0]
    n_asg = n_tok * MOE_TOPK
    rb = MOE_ROW_BLOCK
    cap = -(-(n_asg + n_experts * (rb - 1)) // rb) * rb
    e_flat = top_idx.reshape(n_asg)
    onehot = (e_flat[:, None] == jnp.arange(n_experts, dtype=I32)[None, :]).astype(I32)
    csum = jnp.cumsum(onehot, axis=0)
    rank = jnp.sum(csum * onehot, axis=1) - 1
    counts = csum[-1]
    padded = (counts + rb - 1) // rb * rb
    pad_end = jnp.cumsum(padded)
    pad_start = pad_end - padded
    dest = (jnp.sum(pad_start[None, :] * onehot, axis=1) + rank).astype(I32)
    tok = jnp.arange(n_asg, dtype=I32) // MOE_TOPK
    slot_tok = jnp.full((cap,), zero_row, I32).at[dest].set(tok)
    slot_gate = jnp.zeros((cap,), F32).at[dest].set(top_gate.reshape(n_asg))
    blk_start = jnp.arange(0, cap, rb, dtype=I32)
    blk_expert = jnp.minimum(jnp.searchsorted(pad_end, blk_start, side="right"), n_experts - 1).astype(I32)
    rows_used = pad_end[-1].astype(I32)
    return slot_tok, slot_gate, dest, blk_expert, rows_used


def kernel(x_prompt, x_sample, cache_k0, cache_v0, cache_k1, cache_v1, cache_logf1, page_table, attn_norm, ffn_norm, w_qkv0, q_norm0, k_norm0, w_o0, w_qkvf1, b_f1, q_norm1, k_norm1, w_o1, w_gate0, w_up0, w_down0, w_router1, w_gate1, w_up1, w_down1):
    batch, seq, d_model = x_prompt.shape
    db, dec_seq, _ = x_sample.shape
    assert dec_seq == 1 and db <= SAMPLE_ROWS
    hd = HEAD_DIM
    n_heads = d_model // hd
    n_kv = cache_k0.shape[2]
    assert n_heads == GROUP * n_kv and seq % MOBA_BLOCK == 0
    page = cache_k0.shape[1]
    n_tab = page_table.shape[1]
    past_len = n_tab * page
    assert past_len % MOBA_BLOCK == 0
    n_experts = w_router1.shape[1]
    p_rows = batch * seq
    n_tok = p_rows + db
    mp = p_rows + SAMPLE_ROWS
    nq, nkv = n_heads * hd, n_kv * hd
    page_table = page_table.astype(I32)

    x0 = jnp.concatenate([x_prompt.reshape(p_rows, d_model), x_sample.reshape(db, d_model),
                          jnp.zeros((mp - n_tok, d_model), F32)], axis=0)

    pos = jnp.concatenate([jnp.tile(jnp.arange(seq, dtype=I32), batch), jnp.full((db,), past_len, I32),
                           jnp.zeros((mp - n_tok,), I32)])
    half = hd // 2
    inv_freq = ROPE_THETA ** (-jnp.arange(half, dtype=F32) * 2.0 / hd)
    ang = pos.astype(F32)[:, None] * inv_freq[None, :]
    cos = jnp.concatenate([jnp.cos(ang), jnp.cos(ang)], axis=1)
    sin = jnp.concatenate([-jnp.sin(ang), jnp.sin(ang)], axis=1)

    def place_sample_rows(o, o_s):
        blk = jnp.concatenate([o_s.astype(o.dtype), jnp.zeros((SAMPLE_ROWS - db, o.shape[1]), o.dtype)], axis=0)
        return lax.dynamic_update_slice(o, blk, (p_rows, 0))

    h = _rmsnorm(x0, attn_norm[0])
    qkv = _matmul(h, w_qkv0, nq + 2 * nkv, 512)
    q0, k0, v0 = _qk_post(qkv, q_norm0, k_norm0, n_heads, n_kv, cos=cos, sin=sin)
    o = _attn_prompt(q0, k0, v0, batch, seq, "moba")
    ck0 = cache_k0.reshape(-1, nkv)
    cv0 = cache_v0.reshape(-1, nkv)
    q0_s = q0[p_rows:n_tok].astype(F32).reshape(db, n_heads, hd)
    kmean = _cache_block_means(ck0, page_table, page)
    sel = _moba_select(q0_s, kmean, n_kv)
    o_s = _moba_sample_attend(q0_s, k0[p_rows:n_tok], v0[p_rows:n_tok], ck0, cv0, page_table, sel, page)
    o = place_sample_rows(o, o_s)
    a0 = _matmul(o, w_o0, d_model, 512)

    x1, h = _add_rmsnorm(x0, a0, ffn_norm[0])
    gu_rows = 416
    assert mp % gu_rows == 0
    one_blk = jnp.zeros((mp // gu_rows,), I32)
    n_all = jnp.full((1,), one_blk.shape[0], I32)
    act = _ffn_gate_up(h, w_gate0[None], w_up0[None], one_blk, n_all, gu_rows, 512)
    dn_blk = jnp.zeros((8,), I32)
    f0 = _ffn_down(act, w_down0[None], jnp.ones((mp, 1), F32), dn_blk, jnp.full((1,), 8, I32), mp // 8, 2048, 512)

    x2, h = _add_rmsnorm(x1, f0, attn_norm[1])
    qkv = _matmul(h, w_qkvf1, nq + 2 * nkv, 512)
    fcols = _matmul(h, w_qkvf1, LANES, LANES, col_block0=(nq + 2 * nkv) // LANES)
    b_f = jnp.pad(b_f1.reshape(1, n_heads), ((0, 0), (0, LANES - n_heads)))
    q1, k1, v1, lf_pad = _qk_post(qkv, q_norm1, k_norm1, n_heads, n_kv, fcols=fcols, b_f=b_f)
    logf = lf_pad[:, :n_heads]
    lf_p = logf[:p_rows].reshape(batch, seq, n_heads)
    c_t = _cumsum_lanes(lf_p.transpose(0, 2, 1))
    ck = c_t.reshape(batch, n_kv, GROUP, seq)
    cq = ck.transpose(0, 1, 3, 2)
    o = _attn_prompt(q1, k1, v1, batch, seq, "fox", cq=cq, ck=ck)
    ck1 = cache_k1.reshape(-1, nkv)
    cv1 = cache_v1.reshape(-1, nkv)
    q1_s = q1[p_rows:n_tok].reshape(db, n_heads, hd)
    colg = jnp.arange(nkv, dtype=I32)[None, :] // hd
    rowg = jnp.arange(n_heads, dtype=I32)[:, None] // GROUP
    qbd = jnp.where((colg == rowg)[None], jnp.tile(q1_s, (1, 1, n_kv)), jnp.zeros((), BF16))
    lf_new = jnp.broadcast_to(logf[p_rows:n_tok].reshape(db, n_heads, 1), (db, n_heads, LANES))
    lf_t = _gather_logf_pages(cache_logf1, page_table).transpose(0, 2, 1)
    o_s = _fox_sample_attend(qbd, lf_new, k1[p_rows:n_tok].reshape(db, 1, nkv), v1[p_rows:n_tok].reshape(db, 1, nkv),
                             lf_t, ck1, cv1, page_table, page)
    o = place_sample_rows(o, o_s)
    a1 = _matmul(o, w_o1, d_model, 512)

    x3, ridx, rgate = _add_rmsnorm_router(x2, a1, ffn_norm[1], w_router1)
    slot_tok, slot_gate, dest, blk_expert, rows_used = _moe_plan(
        ridx[:n_tok, :MOE_TOPK], rgate[:n_tok, :MOE_TOPK], n_experts, n_tok)
    rows_used = rows_used.reshape(1)
    blocks_used = rows_used // MOE_ROW_BLOCK
    xs = _moe_gather_norm(x3, ffn_norm[1], slot_tok, rows_used)
    act = _ffn_gate_up(xs, w_gate1, w_up1, blk_expert, blocks_used, MOE_ROW_BLOCK, 512)
    ys = _ffn_down(act, w_down1, slot_gate[:, None], blk_expert, blocks_used, MOE_ROW_BLOCK, 2048, 1024)
    pos_pad = jnp.concatenate([dest, jnp.zeros(((mp - n_tok) * MOE_TOPK,), I32)])
    y = _moe_combine(x3, ys, pos_pad, n_tok)

    def split(t, tail):
        return t[:p_rows].reshape((batch, seq) + tail), t[p_rows:n_tok].reshape((db, 1) + tail)

    y_p, y_s = split(y, (d_model,))
    k0_p, k0_s = split(k0, (n_kv, hd))
    v0_p, v0_s = split(v0, (n_kv, hd))
    k1_p, k1_s = split(k1, (n_kv, hd))
    v1_p, v1_s = split(v1, (n_kv, hd))
    lf_p4, lf_s = split(logf, (n_heads,))
    return (y_p, y_s, k0_p, v0_p, k1_p, v1_p, lf_p4, k0_s, v0_s, k1_s, v1_s, lf_s)
```

```python
import functools

import jax
import jax.numpy as jnp
from jax import lax
from jax.experimental import pallas as pl
from jax.experimental.pallas import tpu as pltpu

F32 = jnp.float32
BF16 = jnp.bfloat16
I32 = jnp.int32

HEAD_DIM = 128
GROUP = 4
MOBA_BLOCK = 256
MOBA_TOPK = 3
MOE_TOPK = 2
ROPE_THETA = 10000.0
EPS = 1e-6
ATTN_SCALE = HEAD_DIM ** -0.5

LANES = 128
V7X_VMEM_BYTES = 64 * 1024 * 1024
COMPILER_SCRATCH_BYTES = 6 * 1024 * 1024
NEG_INF = float("-inf")

SAMPLE_ROWS = 128
MOE_ROW_BLOCK = 512
PAGES_PER_STEP = 8


def _cparams(n_axes, vmem_bytes):
    limit = min(int(vmem_bytes) + COMPILER_SCRATCH_BYTES, V7X_VMEM_BYTES - 2 * 1024 * 1024)
    return pltpu.CompilerParams(
        dimension_semantics=("arbitrary",) * n_axes, vmem_limit_bytes=limit)


def _rms(x, g):
    ms = jnp.mean(x * x, axis=-1, keepdims=True)
    return x * lax.rsqrt(ms + EPS) * g


def _split3(x):
    a1 = x.astype(BF16)
    r1 = x - a1.astype(F32)
    a2 = r1.astype(BF16)
    r2 = r1 - a2.astype(F32)
    return a1, a2, r2.astype(BF16)


def _dot3(parts, m):
    acc = jnp.dot(parts[0], m, preferred_element_type=F32)
    acc = acc + jnp.dot(parts[1], m, preferred_element_type=F32)
    return acc + jnp.dot(parts[2], m, preferred_element_type=F32)


def _rms_body(x_ref, g_ref, h_ref):
    h_ref[...] = _rms(x_ref[...], g_ref[...]).astype(h_ref.dtype)


def _add_rms_body(x_ref, d_ref, g_ref, xo_ref, h_ref):
    x = x_ref[...] + d_ref[...]
    xo_ref[...] = x
    h_ref[...] = _rms(x, g_ref[...]).astype(h_ref.dtype)


def _add_rms_router_body(x_ref, d_ref, g_ref, wr_ref, xo_ref, idx_ref, gate_ref, *, n_experts):
    x = x_ref[...] + d_ref[...]
    xo_ref[...] = x
    h = _rms(x, g_ref[...])
    logits = jnp.dot(h, wr_ref[...], preferred_element_type=F32, precision=lax.Precision.HIGHEST)
    lane = lax.broadcasted_iota(I32, logits.shape, 1)
    s = jnp.where(lane < n_experts, logits, NEG_INF)
    m1 = jnp.max(s, axis=1, keepdims=True)
    i1 = jnp.min(jnp.where(s == m1, lane, LANES), axis=1, keepdims=True)
    s2 = jnp.where(lane == i1, NEG_INF, s)
    m2 = jnp.max(s2, axis=1, keepdims=True)
    i2 = jnp.min(jnp.where(s2 == m2, lane, LANES), axis=1, keepdims=True)
    e = jnp.exp(m2 - m1)
    g1 = 1.0 / (1.0 + e)
    g2 = e / (1.0 + e)
    idx_ref[...] = jnp.where(lane == 0, i1, jnp.where(lane == 1, i2, 0))
    gate_ref[...] = jnp.where(lane == 0, g1, jnp.where(lane == 1, g2, 0.0))


def _row_tile(mp):
    for t in (320, 256, 128):
        if mp % t == 0:
            return t
    raise ValueError(f"unsupported row count {mp}")


def _rmsnorm(x, g):
    mp, d = x.shape
    t = _row_tile(mp)
    return pl.pallas_call(
        _rms_body,
        grid=(mp // t,),
        in_specs=[pl.BlockSpec((t, d), lambda i: (i, 0)), pl.BlockSpec((1, d), lambda i: (0, 0))],
        out_specs=pl.BlockSpec((t, d), lambda i: (i, 0)),
        out_shape=jax.ShapeDtypeStruct((mp, d), BF16),
        compiler_params=_cparams(1, 2 * t * d * 6),
        name="rmsnorm",
    )(x, g.reshape(1, d))


def _add_rmsnorm(x, delta, g):
    mp, d = x.shape
    t = _row_tile(mp)
    row = pl.BlockSpec((t, d), lambda i: (i, 0))
    return pl.pallas_call(
        _add_rms_body,
        grid=(mp // t,),
        in_specs=[row, row, pl.BlockSpec((1, d), lambda i: (0, 0))],
        out_specs=[row, row],
        out_shape=[jax.ShapeDtypeStruct((mp, d), F32), jax.ShapeDtypeStruct((mp, d), BF16)],
        compiler_params=_cparams(1, 2 * t * d * 14),
        name="add_rmsnorm",
    )(x, delta, g.reshape(1, d))


def _add_rmsnorm_router(x, delta, g, w_router):
    mp, d = x.shape
    n_experts = w_router.shape[1]
    t = _row_tile(mp)
    wr = jnp.pad(w_router, ((0, 0), (0, LANES - n_experts)))
    row = pl.BlockSpec((t, d), lambda i: (i, 0))
    small = pl.BlockSpec((t, LANES), lambda i: (i, 0))
    return pl.pallas_call(
        functools.partial(_add_rms_router_body, n_experts=n_experts),
        grid=(mp // t,),
        in_specs=[row, row, pl.BlockSpec((1, d), lambda i: (0, 0)), pl.BlockSpec((d, LANES), lambda i: (0, 0))],
        out_specs=[row, small, small],
        out_shape=[jax.ShapeDtypeStruct((mp, d), F32), jax.ShapeDtypeStruct((mp, LANES), I32),
                   jax.ShapeDtypeStruct((mp, LANES), F32)],
        compiler_params=_cparams(1, 2 * t * d * 12 + 2 * d * LANES * 4 + 8 * t * d),
        name="add_rmsnorm_router",
    )(x, delta, g.reshape(1, d), wr)


def _mm_body(x_ref, w_ref, o_ref, wbf_ref):
    wbf_ref[...] = w_ref[...].astype(BF16)
    o_ref[...] = jnp.dot(x_ref[...], wbf_ref[...], preferred_element_type=F32).astype(o_ref.dtype)


def _matmul(x, w, n_out, bn):
    mp, k = x.shape
    bm = mp // 8
    assert mp % bm == 0 and n_out % bn == 0
    vmem = 2 * (bm * k * 2 + k * bn * 4 + bm * bn * 4) + k * bn * 2
    return pl.pallas_call(
        _mm_body,
        grid=(mp // bm, n_out // bn),
        in_specs=[pl.BlockSpec((bm, k), lambda m, n: (m, 0)),
                  pl.BlockSpec((k, bn), lambda m, n: (0, n))],
        out_specs=pl.BlockSpec((bm, bn), lambda m, n: (m, n)),
        out_shape=jax.ShapeDtypeStruct((mp, n_out), F32),
        scratch_shapes=[pltpu.VMEM((k, bn), BF16)],
        compiler_params=_cparams(2, vmem),
        name="matmul",
    )(x, w)


def _gu_body(be_ref, nu_ref, x_ref, wg_ref, wu_ref, o_ref, wgb_ref, wub_ref):
    b = pl.program_id(1)
    valid = b < nu_ref[0]
    prev = be_ref[jnp.maximum(b - 1, 0)]
    fresh = jnp.logical_or(b == 0, be_ref[b] != prev)

    @pl.when(jnp.logical_and(valid, fresh))
    def _():
        wgb_ref[...] = wg_ref[...].astype(BF16)
        wub_ref[...] = wu_ref[...].astype(BF16)

    @pl.when(valid)
    def _():
        x = x_ref[...]
        g = jnp.dot(x, wgb_ref[...], preferred_element_type=F32)
        u = jnp.dot(x, wub_ref[...], preferred_element_type=F32)
        o_ref[...] = (g * jax.nn.sigmoid(g) * u).astype(o_ref.dtype)

    @pl.when(jnp.logical_not(valid))
    def _():
        o_ref[...] = jnp.zeros(o_ref.shape, o_ref.dtype)


def _ffn_gate_up(x, w_gate, w_up, blk_expert, n_used, bm, bn):
    rows, k = x.shape
    _, _, f = w_gate.shape
    nblk = rows // bm
    assert rows % bm == 0 and f % bn == 0

    def xmap(n, b, be, nu):
        return (jnp.minimum(b, nu[0] - 1), 0)

    def wmap(n, b, be, nu):
        return (be[jnp.minimum(b, nu[0] - 1)], 0, n)

    def omap(n, b, be, nu):
        return (b, n)

    vmem = 2 * (bm * k * 2 + 2 * k * bn * 4 + bm * bn * 2) + 2 * k * bn * 2 + 3 * bm * bn * 4
    return pl.pallas_call(
        _gu_body,
        grid_spec=pltpu.PrefetchScalarGridSpec(
            num_scalar_prefetch=2,
            grid=(f // bn, nblk),
            in_specs=[pl.BlockSpec((bm, k), xmap),
                      pl.BlockSpec((None, k, bn), wmap),
                      pl.BlockSpec((None, k, bn), wmap)],
            out_specs=pl.BlockSpec((bm, bn), omap),
            scratch_shapes=[pltpu.VMEM((k, bn), BF16), pltpu.VMEM((k, bn), BF16)]),
        out_shape=jax.ShapeDtypeStruct((rows, f), BF16),
        compiler_params=_cparams(2, vmem),
        name="ffn_gate_up",
    )(blk_expert, n_used, x, w_gate, w_up)


def _dn_body(fb_ref, cnt_ref, ge_ref, a_ref, w_ref, s_ref, o_ref, wb_ref, acc_ref):
    g, k, j = pl.program_id(0), pl.program_id(2), pl.program_id(3)
    valid = j < cnt_ref[g]
    last = pl.num_programs(2) - 1

    @pl.when(jnp.logical_and(valid, j == 0))
    def _():
        wb_ref[...] = w_ref[...].astype(BF16)

    @pl.when(jnp.logical_and(valid, k == 0))
    def _():
        acc_ref[j] = jnp.zeros(acc_ref.shape[1:], F32)

    @pl.when(valid)
    def _():
        acc_ref[j] += jnp.dot(a_ref[...], wb_ref[...], preferred_element_type=F32)

    @pl.when(jnp.logical_and(valid, k == last))
    def _():
        o_ref[...] = acc_ref[j] * s_ref[...]


def _ffn_down(act, w_down, row_scale, first_blk, n_blk, grp_expert, bm, bn, bk, per_group):
    rows, f = act.shape
    _, _, d = w_down.shape
    n_groups = first_blk.shape[0]
    n_n, n_k = d // bn, f // bk
    assert rows % bm == 0 and d % bn == 0 and f % bk == 0

    def pick(g, n, k, j, fb, cnt):
        empty = cnt[g] == 0
        jj = jnp.minimum(j, jnp.maximum(cnt[g] - 1, 0))
        return fb[g], jj, jnp.where(empty, n_n - 1, n), jnp.where(empty, n_k - 1, k)

    def amap(g, n, k, j, fb, cnt, ge):
        b0, jj, n, k = pick(g, n, k, j, fb, cnt)
        return (b0 + jj, k)

    def wmap(g, n, k, j, fb, cnt, ge):
        b0, jj, n, k = pick(g, n, k, j, fb, cnt)
        return (ge[g], k, n)

    def omap(g, n, k, j, fb, cnt, ge):
        b0, jj, n, k = pick(g, n, k, j, fb, cnt)
        return (b0 + jnp.where(k == n_k - 1, jj, 0), n)

    def smap(g, n, k, j, fb, cnt, ge):
        b0, jj, n, k = pick(g, n, k, j, fb, cnt)
        return (b0 + jnp.where(k == n_k - 1, jj, 0), 0)

    vmem = (2 * (bm * bk * 2 + bk * bn * 4 + bm * bn * 4 + bm * LANES * 4) + bk * bn * 2
            + per_group * bm * bn * 4 + bm * bn * 4)
    return pl.pallas_call(
        _dn_body,
        grid_spec=pltpu.PrefetchScalarGridSpec(
            num_scalar_prefetch=3,
            grid=(n_groups, n_n, n_k, per_group),
            in_specs=[pl.BlockSpec((bm, bk), amap),
                      pl.BlockSpec((None, bk, bn), wmap),
                      pl.BlockSpec((bm, 1), smap)],
            out_specs=pl.BlockSpec((bm, bn), omap),
            scratch_shapes=[pltpu.VMEM((bk, bn), BF16), pltpu.VMEM((per_group, bm, bn), F32)]),
        out_shape=jax.ShapeDtypeStruct((rows, d), F32),
        compiler_params=_cparams(4, vmem),
        name="ffn_down",
    )(first_blk, n_blk, grp_expert, act, w_down, row_scale)


def _block_groups(blocks_per_expert, per_group, n_groups):
    n_experts = blocks_per_expert.shape[0]
    grp_per_e = (blocks_per_expert + per_group - 1) // per_group
    grp_end = jnp.cumsum(grp_per_e)
    blk_start = jnp.cumsum(blocks_per_expert) - blocks_per_expert
    t = jnp.arange(n_groups, dtype=I32)
    e = jnp.minimum(jnp.searchsorted(grp_end, t, side="right"), n_experts - 1).astype(I32)
    local = t - (grp_end - grp_per_e)[e]
    first = blk_start[e] + local * per_group
    cnt = jnp.clip(blocks_per_expert[e] - local * per_group, 0, per_group)
    used = t < grp_end[-1]
    last_t = jnp.maximum(grp_end[-1] - 1, 0)
    last_blk = (first + cnt - 1)[last_t]
    first = jnp.where(used, first, last_blk).astype(I32)
    cnt = jnp.where(used, cnt, 0).astype(I32)
    e = jnp.where(used, e, e[last_t]).astype(I32)
    return first, cnt, e


def _qk_post_body(*refs, n_q, n_kv, rope, forget):
    it = iter(refs)
    qkv_ref = next(it)
    cos_ref = sin_ref = f_ref = bf_ref = lf_ref = None
    if rope:
        cos_ref, sin_ref = next(it), next(it)
    gq_ref, gk_ref = next(it), next(it)
    if forget:
        f_ref, bf_ref = next(it), next(it)
    q_ref, k_ref, v_ref = next(it), next(it), next(it)
    if forget:
        lf_ref = next(it)
    hd = HEAD_DIM
    for h in range(n_q + n_kv):
        t = qkv_ref[:, h * hd:(h + 1) * hd]
        y = _rms(t, gq_ref[...] if h < n_q else gk_ref[...])
        if rope:
            y = y * cos_ref[...] + pltpu.roll(y, hd // 2, 1) * sin_ref[...]
        if h < n_q:
            q_ref[:, h * hd:(h + 1) * hd] = y.astype(q_ref.dtype)
        else:
            k_ref[:, (h - n_q) * hd:(h - n_q + 1) * hd] = y
    v_ref[...] = qkv_ref[:, (n_q + n_kv) * hd:(n_q + 2 * n_kv) * hd]
    if forget:
        z = f_ref[...] + bf_ref[...]
        lf_ref[...] = jnp.minimum(z, 0.0) - jnp.log(1.0 + jnp.exp(-jnp.abs(z)))


def _qk_post(qkv, gq, gk, n_q, n_kv, cos=None, sin=None, fcols=None, b_f=None):
    mp = qkv.shape[0]
    hd = HEAD_DIM
    t = _row_tile(mp)
    rope = cos is not None
    forget = fcols is not None
    row = lambda w: pl.BlockSpec((t, w), lambda i: (i, 0))
    const = lambda w: pl.BlockSpec((1, w), lambda i: (0, 0))
    args, specs = [qkv], [row(qkv.shape[1])]
    if rope:
        args += [cos, sin]
        specs += [row(hd), row(hd)]
    args += [gq.reshape(1, hd), gk.reshape(1, hd)]
    specs += [const(hd), const(hd)]
    if forget:
        args += [fcols, b_f]
        specs += [row(LANES), const(LANES)]
    out_shape = [jax.ShapeDtypeStruct((mp, n_q * hd), BF16), jax.ShapeDtypeStruct((mp, n_kv * hd), F32),
                 jax.ShapeDtypeStruct((mp, n_kv * hd), F32)]
    out_specs = [row(n_q * hd), row(n_kv * hd), row(n_kv * hd)]
    if forget:
        out_shape.append(jax.ShapeDtypeStruct((mp, LANES), F32))
        out_specs.append(row(LANES))
    vmem = 2 * t * (qkv.shape[1] * 4 + n_q * hd * 2 + 2 * n_kv * hd * 4 + 4 * LANES * 4)
    return pl.pallas_call(
        functools.partial(_qk_post_body, n_q=n_q, n_kv=n_kv, rope=rope, forget=forget),
        grid=(mp // t,),
        in_specs=specs,
        out_specs=out_specs,
        out_shape=out_shape,
        compiler_params=_cparams(1, vmem),
        name="qk_post",
    )(*args)


def _attn_prompt_body(*refs, mode, seq, tq, batch):
    o_ref = refs[-1]
    b = pl.program_id(0)

    @pl.when(b < batch)
    def _():
        _attn_prompt_compute(*refs, mode=mode, seq=seq, tq=tq)

    @pl.when(b >= batch)
    def _():
        o_ref[...] = jnp.zeros(o_ref.shape, o_ref.dtype)


def _attn_prompt_compute(*refs, mode, seq, tq):
    if mode == "fox":
        q_ref, k_ref, v_ref, cq_ref, ck_ref, o_ref = refs
    else:
        q_ref, k_ref, v_ref, o_ref = refs
    hd = HEAD_DIM
    nblk = seq // tq
    kf = k_ref[...]
    kb = kf.astype(BF16)
    vb = v_ref[...].astype(BF16)
    row = lax.broadcasted_iota(I32, (tq, tq), 0)
    col = lax.broadcasted_iota(I32, (tq, tq), 1)
    causal = col <= row
    if mode == "moba":
        kmean = jnp.mean(kf.reshape(nblk, tq, hd), axis=1)
        kmean = jnp.concatenate([kmean, jnp.zeros((LANES - nblk, hd), F32)], axis=0)
        lane = lax.broadcasted_iota(I32, (tq, LANES), 1)
        eb = lax.broadcasted_iota(I32, (LANES, seq), 0)
        ec = lax.broadcasted_iota(I32, (LANES, seq), 1)
        expand = jnp.where(jnp.logical_and(ec >= eb * tq, ec < (eb + 1) * tq), 1.0, 0.0).astype(BF16)
    for i in range(nblk):
        n_past = i * tq
        n = n_past + tq
        for hh in range(GROUP):
            qh = q_ref[i * tq:(i + 1) * tq, hh * hd:(hh + 1) * hd]
            s = lax.dot_general(qh, kb[:n], (((1,), (1,)), ((), ())), preferred_element_type=F32) * ATTN_SCALE
            if mode == "fox":
                s = s + (cq_ref[i * tq:(i + 1) * tq, hh:hh + 1] - ck_ref[hh:hh + 1, :n])
            s_diag = jnp.where(causal, s[:, n_past:], NEG_INF)
            if i == 0:
                s = s_diag
            else:
                s_past = s[:, :n_past]
                if mode == "moba" and i > MOBA_TOPK:
                    gate = lax.dot_general(qh.astype(F32), kmean, (((1,), (1,)), ((), ())),
                                           preferred_element_type=F32, precision=lax.Precision.HIGHEST)
                    gm = jnp.where(lane < i, gate, NEG_INF)
                    rank = jnp.zeros((tq, LANES), I32)
                    for d in range(1, i):
                        lower = pltpu.roll(gm, d, 1)
                        higher = pltpu.roll(gm, LANES - d, 1)
                        rank = rank + (lower >= gm).astype(I32) + (higher > gm).astype(I32)
                    sel = jnp.logical_and(rank < MOBA_TOPK, lane < i)
                    picked = jnp.dot(jnp.where(sel, 1.0, 0.0).astype(BF16), expand[:, :n_past],
                                     preferred_element_type=F32)
                    s_past = jnp.where(picked > 0.5, s_past, NEG_INF)
                s = jnp.concatenate([s_past, s_diag], axis=1)
            m = jnp.max(s, axis=1, keepdims=True)
            p = jnp.exp(s - m)
            l = jnp.sum(p, axis=1, keepdims=True)
            o = jnp.dot(p.astype(BF16), vb[:n], preferred_element_type=F32) / l
            o_ref[i * tq:(i + 1) * tq, hh * hd:(hh + 1) * hd] = o.astype(o_ref.dtype)


def _attn_prompt(q, k, v, batch, seq, mode, cq=None, ck=None):
    mp = q.shape[0]
    hd = HEAD_DIM
    n_kv = k.shape[1] // hd
    tq = MOBA_BLOCK
    assert mp > batch * seq and mp - batch * seq <= seq
    src = lambda b, g: (jnp.minimum(b, batch - 1), g)
    src4 = lambda b, g: (jnp.minimum(b, batch - 1), g, 0, 0)
    in_specs = [pl.BlockSpec((seq, GROUP * hd), src),
                pl.BlockSpec((seq, hd), src),
                pl.BlockSpec((seq, hd), src)]
    args = [q, k, v]
    if mode == "fox":
        in_specs += [pl.BlockSpec((None, None, seq, GROUP), src4),
                     pl.BlockSpec((None, None, GROUP, seq), src4)]
        args += [cq, ck]
    vmem = 2 * (2 * seq * GROUP * hd * 2 + 2 * seq * hd * 4 + seq * LANES * 4 + 8 * seq * 4) + 20 * tq * seq * 4
    return pl.pallas_call(
        functools.partial(_attn_prompt_body, mode=mode, seq=seq, tq=tq, batch=batch),
        grid=(batch + 1, n_kv),
        in_specs=in_specs,
        out_specs=pl.BlockSpec((seq, GROUP * hd), lambda b, g: (b, g)),
        out_shape=jax.ShapeDtypeStruct((mp, q.shape[1]), BF16),
        compiler_params=_cparams(2, vmem),
        name=f"attn_prompt_{mode}",
    )(*args)


def _cumsum_body(x_ref, o_ref):
    rows, seq = x_ref.shape
    r = lax.broadcasted_iota(I32, (LANES, LANES), 0)
    c = lax.broadcasted_iota(I32, (LANES, LANES), 1)
    tri = jnp.where(r <= c, 1.0, 0.0).astype(BF16)
    ones = jnp.ones((LANES, LANES), BF16)
    carry = jnp.zeros((rows, LANES), F32)
    for ch in range(seq // LANES):
        parts = _split3(x_ref[:, ch * LANES:(ch + 1) * LANES])
        o_ref[:, ch * LANES:(ch + 1) * LANES] = carry + _dot3(parts, tri)
        carry = carry + _dot3(parts, ones)


def _cumsum_lanes(x):
    b, r, l = x.shape
    spec = pl.BlockSpec((None, r, l), lambda i: (i, 0, 0))
    return pl.pallas_call(
        _cumsum_body, grid=(b,), in_specs=[spec], out_specs=spec,
        out_shape=jax.ShapeDtypeStruct(x.shape, F32),
        compiler_params=_cparams(1, 4 * r * l * 4),
        name="cumsum_lanes",
    )(x)


def _kmean_body(pt_ref, *refs, pages, per_block, n_kv):
    page_refs, o_ref = refs[:pages], refs[pages]
    keys = page_refs[0].shape[0] // n_kv
    for j in range(pages // per_block):
        tot = jnp.sum(page_refs[j * per_block][...].reshape(keys, n_kv, HEAD_DIM), axis=0)
        for t in range(1, per_block):
            tot = tot + jnp.sum(page_refs[j * per_block + t][...].reshape(keys, n_kv, HEAD_DIM), axis=0)
        o_ref[j] = tot * (1.0 / (per_block * keys))


def _cache_block_means(cache_rows, page_table, page, n_kv):
    db, n_pages = page_table.shape
    hd = cache_rows.shape[1]
    per_block = MOBA_BLOCK // page
    pages = PAGES_PER_STEP
    assert n_pages % pages == 0 and pages % per_block == 0
    blocks = pages // per_block

    def pmap(j):
        return lambda b, c, pt: (pt[b, c * pages + j], 0)

    out = pl.pallas_call(
        functools.partial(_kmean_body, pages=pages, per_block=per_block, n_kv=n_kv),
        grid_spec=pltpu.PrefetchScalarGridSpec(
            num_scalar_prefetch=1,
            grid=(db, n_pages // pages),
            in_specs=[pl.BlockSpec((page * n_kv, hd), pmap(j)) for j in range(pages)],
            out_specs=pl.BlockSpec((None, None, blocks, n_kv, hd), lambda b, c, pt: (b, c, 0, 0, 0))),
        out_shape=jax.ShapeDtypeStruct((db, n_pages // pages, blocks, n_kv, hd), F32),
        compiler_params=_cparams(2, 3 * pages * page * n_kv * hd * 4),
        name="cache_block_means",
    )(page_table, *([cache_rows] * pages))
    return out.reshape(db, n_pages // per_block, n_kv, hd)


def _moba_select_body(q_ref, km_ref, o_ref, *, n_kv):
    q = q_ref[...]
    n_heads = q.shape[0]
    nb = km_ref.shape[1]
    rowg = lax.broadcasted_iota(I32, (n_heads, nb), 0) // GROUP
    gate = jnp.zeros((n_heads, nb), F32)
    for g in range(n_kv):
        gg = lax.dot_general(q, km_ref[g], (((1,), (1,)), ((), ())),
                             preferred_element_type=F32, precision=lax.Precision.HIGHEST)
        gate = jnp.where(rowg == g, gg, gate)
    lane = lax.broadcasted_iota(I32, (n_heads, nb), 1)
    olane = lax.broadcasted_iota(I32, (n_heads, LANES), 1)
    out = jnp.zeros((n_heads, LANES), I32)
    for j in range(MOBA_TOPK):
        m = jnp.max(gate, axis=1, keepdims=True)
        idx = jnp.min(jnp.where(gate == m, lane, nb), axis=1, keepdims=True)
        out = jnp.where(olane == j, idx, out)
        gate = jnp.where(lane == idx, NEG_INF, gate)
    o_ref[...] = out


def _moba_select(q_s, kmean):
    db, n_heads, hd = q_s.shape
    n_kv, nb = kmean.shape[1:3]
    out = pl.pallas_call(
        functools.partial(_moba_select_body, n_kv=n_kv),
        grid=(db,),
        in_specs=[pl.BlockSpec((None, n_heads, hd), lambda b: (b, 0, 0)),
                  pl.BlockSpec((None, n_kv, nb, hd), lambda b: (b, 0, 0, 0))],
        out_specs=pl.BlockSpec((None, n_heads, LANES), lambda b: (b, 0, 0)),
        out_shape=jax.ShapeDtypeStruct((db, n_heads, LANES), I32),
        compiler_params=_cparams(1, 4 * n_kv * nb * hd * 4),
        name="moba_select",
    )(q_s, kmean)
    return out[:, :, :MOBA_TOPK]


def _moba_sample_body(pt_ref, sel_ref, q_ref, kn_ref, vn_ref, *refs, n_pages, n_kv):
    k_refs, v_refs, o_ref = refs[:n_pages], refs[n_pages:2 * n_pages], refs[2 * n_pages]
    q = q_ref[...]
    q8 = jnp.broadcast_to(q, (8, HEAD_DIM)).astype(BF16)
    g = pl.program_id(1) // GROUP
    lane = lax.broadcasted_iota(I32, (1, k_refs[0].shape[0]), 1)
    mine = lane % n_kv == g
    scores = [jnp.where(mine, lax.dot_general(q8, kr[...].astype(BF16), (((1,), (1,)), ((), ())),
                                              preferred_element_type=F32)[0:1] * ATTN_SCALE, NEG_INF)
              for kr in k_refs]
    s_own = jnp.sum(q * kn_ref[...], axis=1, keepdims=True) * ATTN_SCALE
    m = s_own
    for s in scores:
        m = jnp.maximum(m, jnp.max(s, axis=1, keepdims=True))
    p_own = jnp.exp(s_own - m)
    l = p_own
    acc = p_own * vn_ref[...]
    for s, vr in zip(scores, v_refs):
        p = jnp.exp(s - m)
        l = l + jnp.sum(p, axis=1, keepdims=True)
        p8 = jnp.broadcast_to(p, (8, p.shape[1])).astype(BF16)
        acc = acc + jnp.dot(p8, vr[...].astype(BF16), preferred_element_type=F32)[0:1]
    o_ref[...] = acc / l


def _moba_sample_attend(q_s, k_new, v_new, ck_rows, cv_rows, page_table, sel, page, n_kv):
    db, n_heads, hd = q_s.shape
    per_block = MOBA_BLOCK // page
    n_pages = MOBA_TOPK * per_block

    def pmap(j):
        blk, t = j // per_block, j % per_block
        return lambda b, h, pt, sl: (pt[b, sl[b, h * MOBA_TOPK + blk] * per_block + t], 0)

    one = lambda b, h, pt, sl: (b, h, 0, 0)
    kv_one = lambda b, h, pt, sl: (b, h // GROUP, 0, 0)
    row = (None, None, 1, hd)
    page_spec = [pl.BlockSpec((page * n_kv, hd), pmap(j)) for j in range(n_pages)]
    out = pl.pallas_call(
        functools.partial(_moba_sample_body, n_pages=n_pages, n_kv=n_kv),
        grid_spec=pltpu.PrefetchScalarGridSpec(
            num_scalar_prefetch=2,
            grid=(db, n_heads),
            in_specs=[pl.BlockSpec(row, one), pl.BlockSpec(row, kv_one), pl.BlockSpec(row, kv_one)]
            + page_spec + page_spec,
            out_specs=pl.BlockSpec(row, one)),
        out_shape=jax.ShapeDtypeStruct((db, n_heads, 1, hd), F32),
        compiler_params=_cparams(2, 3 * 2 * n_pages * page * n_kv * hd * 4),
        name="moba_sample_attend",
    )(page_table, sel.reshape(db, n_heads * MOBA_TOPK), q_s.reshape(db, n_heads, 1, hd),
      k_new.reshape(db, n_kv, 1, hd), v_new.reshape(db, n_kv, 1, hd),
      *([ck_rows] * n_pages), *([cv_rows] * n_pages))
    return out.reshape(db, n_heads * hd)


def _page_gather_body(pt_ref, *refs, pages):
    o_ref = refs[pages]
    rows = refs[0].shape[0]
    for j in range(pages):
        o_ref[j * rows:(j + 1) * rows, :] = refs[j][...]


def _gather_logf_pages(cache_logf, page_table):
    db, n_tab = page_table.shape
    _, page, n_heads = cache_logf.shape
    pages = 2 * PAGES_PER_STEP
    assert n_tab % pages == 0

    def pmap(j):
        return lambda b, c, pt: (pt[b, c * pages + j], 0, 0)

    return pl.pallas_call(
        functools.partial(_page_gather_body, pages=pages),
        grid_spec=pltpu.PrefetchScalarGridSpec(
            num_scalar_prefetch=1,
            grid=(db, n_tab // pages),
            in_specs=[pl.BlockSpec((None, page, n_heads), pmap(j)) for j in range(pages)],
            out_specs=pl.BlockSpec((None, pages * page, n_heads), lambda b, c, pt: (b, c, 0))),
        out_shape=jax.ShapeDtypeStruct((db, n_tab * page, n_heads), F32),
        compiler_params=_cparams(2, 4 * pages * page * LANES * 4),
        name="gather_logf_pages",
    )(page_table, *([cache_logf] * pages))


def _suffix_body(x_ref, init_ref, o_ref, r_ref):
    c = pl.program_id(1)

    @pl.when(c == 0)
    def _():
        r_ref[...] = init_ref[...]

    rr = lax.broadcasted_iota(I32, (LANES, LANES), 0)
    cc = lax.broadcasted_iota(I32, (LANES, LANES), 1)
    later = jnp.where(rr > cc, 1.0, 0.0).astype(BF16)
    ones = jnp.ones((LANES, LANES), BF16)
    for j in range(x_ref.shape[1] // LANES - 1, -1, -1):
        parts = _split3(x_ref[:, j * LANES:(j + 1) * LANES])
        r = r_ref[...]
        o_ref[:, j * LANES:(j + 1) * LANES] = r + _dot3(parts, later)
        r_ref[...] = r + _dot3(parts, ones)


def _suffix_sum_lanes(x, init, chunk=2048):
    b, r, l = x.shape
    n_chunks = l // chunk
    assert l % chunk == 0
    rev = lambda i, c: (i, 0, n_chunks - 1 - c)
    return pl.pallas_call(
        _suffix_body,
        grid=(b, n_chunks),
        in_specs=[pl.BlockSpec((None, r, chunk), rev), pl.BlockSpec((None, r, LANES), lambda i, c: (i, 0, 0))],
        out_specs=pl.BlockSpec((None, r, chunk), rev),
        out_shape=jax.ShapeDtypeStruct(x.shape, F32),
        scratch_shapes=[pltpu.VMEM((r, LANES), F32)],
        compiler_params=_cparams(2, 8 * r * chunk * 4),
        name="suffix_sum_lanes",
    )(x, init)


def _fox_sample_body(pt_ref, q_ref, kn_ref, vn_ref, bias_ref, *refs, pages, n_kv):
    k_refs, v_refs = refs[:pages], refs[pages:2 * pages]
    o_ref, m_ref, l_ref, acc_ref = refs[2 * pages:]
    c = pl.program_id(1)
    n_heads = q_ref.shape[0]
    rows = k_refs[0].shape[0]

    @pl.when(c == 0)
    def _():
        m_ref[...] = jnp.full(m_ref.shape, NEG_INF, F32)
        l_ref[...] = jnp.zeros(l_ref.shape, F32)
        acc_ref[...] = jnp.zeros(acc_ref.shape, F32)

    q = q_ref[...]

    def absorb(s, vb):
        m_old = m_ref[...]
        m_new = jnp.maximum(m_old, jnp.max(s, axis=1, keepdims=True))
        alpha = jnp.exp(m_old - m_new)
        p = jnp.exp(s - m_new[:, 0:1])
        l_ref[...] = alpha * l_ref[...] + jnp.sum(p, axis=1, keepdims=True)
        acc_ref[...] = acc_ref[...] * alpha + jnp.dot(p.astype(BF16), vb, preferred_element_type=F32)
        m_ref[...] = m_new

    for j in range(pages):
        s = lax.dot_general(q, k_refs[j][...].astype(BF16), (((1,), (1,)), ((), ())),
                            preferred_element_type=F32) * ATTN_SCALE + bias_ref[:, j * rows:(j + 1) * rows]
        absorb(s, v_refs[j][...].astype(BF16))

    @pl.when(c == pl.num_programs(1) - 1)
    def _():
        s = lax.dot_general(q, kn_ref[...].astype(BF16), (((1,), (1,)), ((), ())),
                            preferred_element_type=F32) * ATTN_SCALE
        rowg = lax.broadcasted_iota(I32, s.shape, 0) // GROUP
        lane = lax.broadcasted_iota(I32, s.shape, 1)
        absorb(jnp.where(lane == rowg, s, NEG_INF), vn_ref[...].astype(BF16))
        o_ref[...] = acc_ref[...] / l_ref[...]


def _fox_sample_attend(q_s, k_new, v_new, bias, ck_rows, cv_rows, page_table, page, n_kv):
    db, n_heads, hd = q_s.shape
    n_tab = page_table.shape[1]
    pages = PAGES_PER_STEP
    n_chunks = n_tab // pages
    rows = page * n_kv
    assert n_tab % pages == 0 and k_new.shape[1] == LANES

    def pmap(j):
        return lambda b, c, pt: (pt[b, c * pages + j], 0)

    per_b = lambda b, c, pt: (b, 0, 0)
    page_spec = [pl.BlockSpec((rows, hd), pmap(j)) for j in range(pages)]
    out = pl.pallas_call(
        functools.partial(_fox_sample_body, pages=pages, n_kv=n_kv),
        grid_spec=pltpu.PrefetchScalarGridSpec(
            num_scalar_prefetch=1,
            grid=(db, n_chunks),
            in_specs=[pl.BlockSpec((None, n_heads, hd), per_b),
                      pl.BlockSpec((None, LANES, hd), per_b),
                      pl.BlockSpec((None, LANES, hd), per_b),
                      pl.BlockSpec((None, n_heads, pages * rows), lambda b, c, pt: (b, 0, c))]
            + page_spec + page_spec,
            out_specs=pl.BlockSpec((None, n_heads, hd), per_b),
            scratch_shapes=[pltpu.VMEM((n_heads, LANES), F32), pltpu.VMEM((n_heads, LANES), F32),
                            pltpu.VMEM((n_heads, hd), F32)]),
        out_shape=jax.ShapeDtypeStruct((db, n_heads, hd), F32),
        compiler_params=_cparams(2, 3 * 2 * pages * rows * hd * 4 + 4 * n_heads * pages * rows * 4),
        name="fox_sample_attend",
    )(page_table, q_s, k_new, v_new, bias, *([ck_rows] * pages), *([cv_rows] * pages))
    return out.reshape(db, n_heads * hd)


def _row_copy(src_hbm, row, dst, slot, sem):
    return pltpu.make_async_copy(src_hbm.at[pl.ds(row, 1)], dst.at[pl.ds(slot, 1)], sem)


def _gather_norm_body(tok_ref, nrows_ref, x_hbm, g_ref, o_ref, buf_ref, sem, *, rows):
    i = pl.program_id(0)
    base = i * rows

    @pl.when(base < nrows_ref[0])
    def _():
        def issue(r, carry):
            _row_copy(x_hbm, tok_ref[base + r], buf_ref, r, sem).start()
            return carry

        def wait(r, carry):
            _row_copy(x_hbm, tok_ref[base + r], buf_ref, r, sem).wait()
            return carry

        lax.fori_loop(0, rows, issue, 0)
        lax.fori_loop(0, rows, wait, 0)
        o_ref[...] = _rms(buf_ref[...], g_ref[...]).astype(o_ref.dtype)

    @pl.when(base >= nrows_ref[0])
    def _():
        o_ref[...] = jnp.zeros(o_ref.shape, o_ref.dtype)


def _moe_gather_norm(x, g, slot_tok, n_rows_used, rows=256):
    cap = slot_tok.shape[0]
    d = x.shape[1]
    assert cap % rows == 0
    return pl.pallas_call(
        functools.partial(_gather_norm_body, rows=rows),
        grid_spec=pltpu.PrefetchScalarGridSpec(
            num_scalar_prefetch=2,
            grid=(cap // rows,),
            in_specs=[pl.BlockSpec(memory_space=pl.ANY), pl.BlockSpec((1, d), lambda i, t, n: (0, 0))],
            out_specs=pl.BlockSpec((rows, d), lambda i, t, n: (i, 0)),
            scratch_shapes=[pltpu.VMEM((rows, d), F32), pltpu.SemaphoreType.DMA(())]),
        out_shape=jax.ShapeDtypeStruct((cap, d), BF16),
        compiler_params=_cparams(1, rows * d * 4 * 3 + 2 * rows * d * 2),
        name="moe_gather_norm",
    )(slot_tok, n_rows_used, x, g.reshape(1, d))


def _combine_body(pos_ref, x_ref, y_hbm, o_ref, buf_ref, sem, *, rows, n_tok):
    i = pl.program_id(0)
    base = i * rows

    def issue(r, carry):
        for j in range(MOE_TOPK):
            _row_copy(y_hbm, pos_ref[(base + r) * MOE_TOPK + j], buf_ref.at[j], r, sem).start()
        return carry

    def wait(r, carry):
        for j in range(MOE_TOPK):
            _row_copy(y_hbm, pos_ref[(base + r) * MOE_TOPK + j], buf_ref.at[j], r, sem).wait()
        return carry

    lax.fori_loop(0, rows, issue, 0)
    lax.fori_loop(0, rows, wait, 0)
    tok = base + lax.broadcasted_iota(I32, (rows, 1), 0)
    y = buf_ref[0]
    for j in range(1, MOE_TOPK):
        y = y + buf_ref[j]
    o_ref[...] = x_ref[...] + jnp.where(tok < n_tok, y, 0.0)


def _moe_combine(x, y, pos, n_tok, rows=128):
    mp, d = x.shape
    assert mp % rows == 0
    return pl.pallas_call(
        functools.partial(_combine_body, rows=rows, n_tok=n_tok),
        grid_spec=pltpu.PrefetchScalarGridSpec(
            num_scalar_prefetch=1,
            grid=(mp // rows,),
            in_specs=[pl.BlockSpec((rows, d), lambda i, p: (i, 0)), pl.BlockSpec(memory_space=pl.ANY)],
            out_specs=pl.BlockSpec((rows, d), lambda i, p: (i, 0)),
            scratch_shapes=[pltpu.VMEM((MOE_TOPK, rows, d), F32), pltpu.SemaphoreType.DMA(())]),
        out_shape=jax.ShapeDtypeStruct((mp, d), F32),
        compiler_params=_cparams(1, (MOE_TOPK + 4) * rows * d * 4),
        name="moe_combine",
    )(pos, x, y)


def _moe_plan(top_idx, top_gate, n_experts, zero_row):
    n_tok = top_idx.shape[0]
    n_asg = n_tok * MOE_TOPK
    rb = MOE_ROW_BLOCK
    cap = -(-(n_asg + n_experts * (rb - 1)) // rb) * rb
    e_flat = top_idx.reshape(n_asg)
    onehot = (e_flat[:, None] == jnp.arange(n_experts, dtype=I32)[None, :]).astype(I32)
    csum = jnp.cumsum(onehot, axis=0)
    rank = jnp.sum(csum * onehot, axis=1) - 1
    counts = csum[-1]
    padded = (counts + rb - 1) // rb * rb
    pad_end = jnp.cumsum(padded)
    pad_start = pad_end - padded
    dest = (jnp.sum(pad_start[None, :] * onehot, axis=1) + rank).astype(I32)
    tok = jnp.arange(n_asg, dtype=I32) // MOE_TOPK
    slot_tok = jnp.full((cap,), zero_row, I32).at[dest].set(tok)
    slot_gate = jnp.zeros((cap,), F32).at[dest].set(top_gate.reshape(n_asg))
    blk_start = jnp.arange(0, cap, rb, dtype=I32)
    blk_expert = jnp.minimum(jnp.searchsorted(pad_end, blk_start, side="right"), n_experts - 1).astype(I32)
    rows_used = pad_end[-1].astype(I32)
    return slot_tok, slot_gate, dest, blk_expert, rows_used, (padded // rb).astype(I32)


def kernel(x_prompt, x_sample, cache_k0, cache_v0, cache_k1, cache_v1, cache_logf1, page_table, attn_norm, ffn_norm, w_qkv0, q_norm0, k_norm0, w_o0, w_qkvf1, b_f1, q_norm1, k_norm1, w_o1, w_gate0, w_up0, w_down0, w_router1, w_gate1, w_up1, w_down1):
    batch, seq, d_model = x_prompt.shape
    db, dec_seq, _ = x_sample.shape
    assert dec_seq == 1 and db <= SAMPLE_ROWS
    hd = HEAD_DIM
    n_heads = d_model // hd
    n_kv = cache_k0.shape[2]
    assert n_heads == GROUP * n_kv and seq % MOBA_BLOCK == 0
    page = cache_k0.shape[1]
    n_tab = page_table.shape[1]
    past_len = n_tab * page
    assert past_len % MOBA_BLOCK == 0
    n_experts = w_router1.shape[1]
    p_rows = batch * seq
    n_tok = p_rows + db
    mp = p_rows + SAMPLE_ROWS
    nq, nkv = n_heads * hd, n_kv * hd
    page_table = page_table.astype(I32)

    x0 = jnp.concatenate([x_prompt.reshape(p_rows, d_model), x_sample.reshape(db, d_model),
                          jnp.zeros((mp - n_tok, d_model), F32)], axis=0)

    pos = jnp.concatenate([jnp.tile(jnp.arange(seq, dtype=I32), batch), jnp.full((db,), past_len, I32),
                           jnp.zeros((mp - n_tok,), I32)])
    half = hd // 2
    inv_freq = ROPE_THETA ** (-jnp.arange(half, dtype=F32) * 2.0 / hd)
    ang = pos.astype(F32)[:, None] * inv_freq[None, :]
    cos = jnp.concatenate([jnp.cos(ang), jnp.cos(ang)], axis=1)
    sin = jnp.concatenate([-jnp.sin(ang), jnp.sin(ang)], axis=1)

    def place_sample_rows(o, o_s):
        blk = jnp.concatenate([o_s.astype(o.dtype), jnp.zeros((SAMPLE_ROWS - db, o.shape[1]), o.dtype)], axis=0)
        return lax.dynamic_update_slice(o, blk, (p_rows, 0))

    h = _rmsnorm(x0, attn_norm[0])
    qkv = _matmul(h, w_qkv0, nq + 2 * nkv, 512)
    q0, k0, v0 = _qk_post(qkv, q_norm0, k_norm0, n_heads, n_kv, cos=cos, sin=sin)
    o = _attn_prompt(q0, k0, v0, batch, seq, "moba")
    ck0 = cache_k0.reshape(-1, hd)
    cv0 = cache_v0.reshape(-1, hd)
    q0_s = q0[p_rows:n_tok].astype(F32).reshape(db, n_heads, hd)
    kmean = _cache_block_means(ck0, page_table, page, n_kv).transpose(0, 2, 1, 3)
    sel = _moba_select(q0_s, kmean)
    o_s = _moba_sample_attend(q0_s, k0[p_rows:n_tok], v0[p_rows:n_tok], ck0, cv0, page_table, sel, page, n_kv)
    o = place_sample_rows(o, o_s)
    a0 = _matmul(o, w_o0, d_model, 512)

    x1, h = _add_rmsnorm(x0, a0, ffn_norm[0])
    gu_rows = 416
    assert mp % gu_rows == 0
    one_blk = jnp.zeros((mp // gu_rows,), I32)
    n_all = jnp.full((1,), one_blk.shape[0], I32)
    act = _ffn_gate_up(h, w_gate0[None], w_up0[None], one_blk, n_all, gu_rows, 512)
    dn_rows, dn_group = 832, 5
    assert mp % (dn_rows * dn_group) == 0
    dn_groups = mp // (dn_rows * dn_group)
    f0 = _ffn_down(act, w_down0[None], jnp.ones((mp, 1), F32),
                   jnp.arange(dn_groups, dtype=I32) * dn_group, jnp.full((dn_groups,), dn_group, I32),
                   jnp.zeros((dn_groups,), I32), dn_rows, 1024, 2048, dn_group)

    x2, h = _add_rmsnorm(x1, f0, attn_norm[1])
    qkv = _matmul(h, w_qkvf1, nq + 2 * nkv, 512)
    w_f = jnp.pad(w_qkvf1[:, nq + 2 * nkv:], ((0, 0), (0, LANES - n_heads)))
    fcols = _matmul(h, w_f, LANES, LANES)
    b_f = jnp.pad(b_f1.reshape(1, n_heads), ((0, 0), (0, LANES - n_heads)))
    q1, k1, v1, lf_pad = _qk_post(qkv, q_norm1, k_norm1, n_heads, n_kv, fcols=fcols, b_f=b_f)
    logf = lf_pad[:, :n_heads]
    lf_p = logf[:p_rows].reshape(batch, seq, n_heads)
    c_t = _cumsum_lanes(lf_p.transpose(0, 2, 1))
    ck = c_t.reshape(batch, n_kv, GROUP, seq)
    cq = ck.transpose(0, 1, 3, 2)
    o = _attn_prompt(q1, k1, v1, batch, seq, "fox", cq=cq, ck=ck)
    ck1 = cache_k1.reshape(-1, hd)
    cv1 = cache_v1.reshape(-1, hd)
    q1_s = q1[p_rows:n_tok].reshape(db, n_heads, hd)
    lf_new = jnp.broadcast_to(logf[p_rows:n_tok].reshape(db, n_heads, 1), (db, n_heads, LANES))
    lf_t = _gather_logf_pages(cache_logf1, page_table).transpose(0, 2, 1)
    fbias = _suffix_sum_lanes(lf_t, lf_new)
    own_head = (jnp.arange(n_kv, dtype=I32)[None, :] == jnp.arange(n_heads, dtype=I32)[:, None] // GROUP)
    fbias = jnp.where(own_head[None, :, None, :], fbias[:, :, :, None], NEG_INF).reshape(db, n_heads, past_len * n_kv)
    pad_new = lambda t: jnp.pad(t[p_rows:n_tok].reshape(db, n_kv, hd), ((0, 0), (0, LANES - n_kv), (0, 0)))
    o_s = _fox_sample_attend(q1_s, pad_new(k1), pad_new(v1), fbias, ck1, cv1, page_table, page, n_kv)
    o = place_sample_rows(o, o_s)
    a1 = _matmul(o, w_o1, d_model, 512)

    x3, ridx, rgate = _add_rmsnorm_router(x2, a1, ffn_norm[1], w_router1)
    slot_tok, slot_gate, dest, blk_expert, rows_used, blocks_per_expert = _moe_plan(
        ridx[:n_tok, :MOE_TOPK], rgate[:n_tok, :MOE_TOPK], n_experts, n_tok)
    rows_used = rows_used.reshape(1)
    blocks_used = rows_used // MOE_ROW_BLOCK
    xs = _moe_gather_norm(x3, ffn_norm[1], slot_tok, rows_used)
    act = _ffn_gate_up(xs, w_gate1, w_up1, blk_expert, blocks_used, MOE_ROW_BLOCK, 512)
    moe_group = 5
    n_groups = n_experts + (xs.shape[0] // MOE_ROW_BLOCK) // moe_group
    first_blk, n_blk, grp_expert = _block_groups(blocks_per_expert, moe_group, n_groups)
    ys = _ffn_down(act, w_down1, slot_gate[:, None], first_blk, n_blk, grp_expert,
                   MOE_ROW_BLOCK, 1024, 2048, moe_group)
    pos_pad = jnp.concatenate([dest, jnp.zeros(((mp - n_tok) * MOE_TOPK,), I32)])
    y = _moe_combine(x3, ys, pos_pad, n_tok)

    def split(t, tail):
        return t[:p_rows].reshape((batch, seq) + tail), t[p_rows:n_tok].reshape((db, 1) + tail)

    y_p, y_s = split(y, (d_model,))
    k0_p, k0_s = split(k0, (n_kv, hd))
    v0_p, v0_s = split(v0, (n_kv, hd))
    k1_p, k1_s = split(k1, (n_kv, hd))
    v1_p, v1_s = split(v1, (n_kv, hd))
    lf_p4, lf_s = split(logf, (n_heads,))
    return (y_p, y_s, k0_p, v0_p, k1_p, v1_p, lf_p4, k0_s, v0_s, k1_s, v1_s, lf_s)
```

```python
import functools

import jax
import jax.numpy as jnp
from jax import lax
from jax.experimental import pallas as pl
from jax.experimental.pallas import tpu as pltpu

F32 = jnp.float32
BF16 = jnp.bfloat16
I32 = jnp.int32

HEAD_DIM = 128
GROUP = 4
MOBA_BLOCK = 256
MOBA_TOPK = 3
MOE_TOPK = 2
ROPE_THETA = 10000.0
EPS = 1e-6
ATTN_SCALE = HEAD_DIM ** -0.5

LANES = 128
V7X_VMEM_BYTES = 64 * 1024 * 1024
COMPILER_SCRATCH_BYTES = 6 * 1024 * 1024
NEG_INF = float("-inf")

SAMPLE_ROWS = 128
MOE_SUPER_ROWS = 2560
MOE_SUB_BLOCKS = 5
PAGES_PER_STEP = 8


def _cparams(n_axes, vmem_bytes):
    limit = min(int(vmem_bytes) + COMPILER_SCRATCH_BYTES, V7X_VMEM_BYTES - 2 * 1024 * 1024)
    return pltpu.CompilerParams(
        dimension_semantics=("arbitrary",) * n_axes, vmem_limit_bytes=limit)


def _rms(x, g):
    ms = jnp.mean(x * x, axis=-1, keepdims=True)
    return x * lax.rsqrt(ms + EPS) * g


def _split3(x):
    a1 = x.astype(BF16)
    r1 = x - a1.astype(F32)
    a2 = r1.astype(BF16)
    r2 = r1 - a2.astype(F32)
    return a1, a2, r2.astype(BF16)


def _dot3(parts, m):
    acc = jnp.dot(parts[0], m, preferred_element_type=F32)
    acc = acc + jnp.dot(parts[1], m, preferred_element_type=F32)
    return acc + jnp.dot(parts[2], m, preferred_element_type=F32)


def _rms_body(x_ref, g_ref, h_ref):
    h_ref[...] = _rms(x_ref[...], g_ref[...]).astype(h_ref.dtype)


def _add_rms_body(x_ref, d_ref, g_ref, xo_ref, h_ref):
    x = x_ref[...] + d_ref[...]
    xo_ref[...] = x
    h_ref[...] = _rms(x, g_ref[...]).astype(h_ref.dtype)


def _add_rms_router_body(x_ref, d_ref, g_ref, wr_ref, xo_ref, idx_ref, gate_ref, *, n_experts):
    x = x_ref[...] + d_ref[...]
    xo_ref[...] = x
    h = _rms(x, g_ref[...])
    logits = jnp.dot(h, wr_ref[...], preferred_element_type=F32, precision=lax.Precision.HIGHEST)
    lane = lax.broadcasted_iota(I32, logits.shape, 1)
    s = jnp.where(lane < n_experts, logits, NEG_INF)
    m1 = jnp.max(s, axis=1, keepdims=True)
    i1 = jnp.min(jnp.where(s == m1, lane, LANES), axis=1, keepdims=True)
    s2 = jnp.where(lane == i1, NEG_INF, s)
    m2 = jnp.max(s2, axis=1, keepdims=True)
    i2 = jnp.min(jnp.where(s2 == m2, lane, LANES), axis=1, keepdims=True)
    e = jnp.exp(m2 - m1)
    g1 = 1.0 / (1.0 + e)
    g2 = e / (1.0 + e)
    idx_ref[...] = jnp.where(lane == 0, i1, jnp.where(lane == 1, i2, 0))
    gate_ref[...] = jnp.where(lane == 0, g1, jnp.where(lane == 1, g2, 0.0))


def _row_tile(mp):
    for t in (320, 256, 128):
        if mp % t == 0:
            return t
    raise ValueError(f"unsupported row count {mp}")


def _rmsnorm(x, g):
    mp, d = x.shape
    t = _row_tile(mp)
    return pl.pallas_call(
        _rms_body,
        grid=(mp // t,),
        in_specs=[pl.BlockSpec((t, d), lambda i: (i, 0)), pl.BlockSpec((1, d), lambda i: (0, 0))],
        out_specs=pl.BlockSpec((t, d), lambda i: (i, 0)),
        out_shape=jax.ShapeDtypeStruct((mp, d), BF16),
        compiler_params=_cparams(1, 2 * t * d * 6),
        name="rmsnorm",
    )(x, g.reshape(1, d))


def _add_rmsnorm(x, delta, g):
    mp, d = x.shape
    t = _row_tile(mp)
    row = pl.BlockSpec((t, d), lambda i: (i, 0))
    return pl.pallas_call(
        _add_rms_body,
        grid=(mp // t,),
        in_specs=[row, row, pl.BlockSpec((1, d), lambda i: (0, 0))],
        out_specs=[row, row],
        out_shape=[jax.ShapeDtypeStruct((mp, d), F32), jax.ShapeDtypeStruct((mp, d), BF16)],
        compiler_params=_cparams(1, 2 * t * d * 14),
        name="add_rmsnorm",
    )(x, delta, g.reshape(1, d))


def _add_rmsnorm_router(x, delta, g, w_router):
    mp, d = x.shape
    n_experts = w_router.shape[1]
    t = _row_tile(mp)
    wr = jnp.pad(w_router, ((0, 0), (0, LANES - n_experts)))
    row = pl.BlockSpec((t, d), lambda i: (i, 0))
    small = pl.BlockSpec((t, LANES), lambda i: (i, 0))
    return pl.pallas_call(
        functools.partial(_add_rms_router_body, n_experts=n_experts),
        grid=(mp // t,),
        in_specs=[row, row, pl.BlockSpec((1, d), lambda i: (0, 0)), pl.BlockSpec((d, LANES), lambda i: (0, 0))],
        out_specs=[row, small, small],
        out_shape=[jax.ShapeDtypeStruct((mp, d), F32), jax.ShapeDtypeStruct((mp, LANES), I32),
                   jax.ShapeDtypeStruct((mp, LANES), F32)],
        compiler_params=_cparams(1, 2 * t * d * 12 + 2 * d * LANES * 4 + 8 * t * d),
        name="add_rmsnorm_router",
    )(x, delta, g.reshape(1, d), wr)


def _mm_body(x_ref, w_ref, o_ref, wbf_ref):
    wbf_ref[...] = w_ref[...].astype(BF16)
    o_ref[...] = jnp.dot(x_ref[...], wbf_ref[...], preferred_element_type=F32).astype(o_ref.dtype)


def _matmul(x, w, n_out, bn):
    mp, k = x.shape
    bm = mp // 8
    assert mp % bm == 0 and n_out % bn == 0
    vmem = 2 * (bm * k * 2 + k * bn * 4 + bm * bn * 4) + k * bn * 2
    return pl.pallas_call(
        _mm_body,
        grid=(mp // bm, n_out // bn),
        in_specs=[pl.BlockSpec((bm, k), lambda m, n: (m, 0)),
                  pl.BlockSpec((k, bn), lambda m, n: (0, n))],
        out_specs=pl.BlockSpec((bm, bn), lambda m, n: (m, n)),
        out_shape=jax.ShapeDtypeStruct((mp, n_out), F32),
        scratch_shapes=[pltpu.VMEM((k, bn), BF16)],
        compiler_params=_cparams(2, vmem),
        name="matmul",
    )(x, w)


def _gu_body(src_ref, ex_ref, ns_ref, x_ref, wg_ref, wu_ref, o_ref, wgb_ref, wub_ref, *, n_sub, sub):
    ns = ns_ref[pl.program_id(0)]

    @pl.when(ns > 0)
    def _():
        wgb_ref[...] = wg_ref[...].astype(BF16)
        wub_ref[...] = wu_ref[...].astype(BF16)

    for j in range(n_sub):
        rows = slice(j * sub, (j + 1) * sub)

        @pl.when(j < ns)
        def _():
            x = x_ref[rows, :]
            g = jnp.dot(x, wgb_ref[...], preferred_element_type=F32)
            u = jnp.dot(x, wub_ref[...], preferred_element_type=F32)
            o_ref[rows, :] = (g * jax.nn.sigmoid(g) * u).astype(o_ref.dtype)

        @pl.when(jnp.logical_and(j >= ns, ns > 0))
        def _():
            o_ref[rows, :] = jnp.zeros((sub, o_ref.shape[1]), o_ref.dtype)


def _ffn_gate_up(x, w_gate, w_up, sb_src, sb_expert, sb_nsub, sb_rows, n_sub, bn):
    rows, k = x.shape
    _, _, f = w_gate.shape
    n_sb = rows // sb_rows
    n_n = f // bn
    sub = sb_rows // n_sub
    assert rows % sb_rows == 0 and f % bn == 0 and sb_rows % n_sub == 0

    def col(s, n, ns):
        return jnp.where(ns[s] == 0, n_n - 1, n)

    def xmap(s, n, src, ex, ns):
        return (src[s], 0)

    def wmap(s, n, src, ex, ns):
        return (ex[s], 0, col(s, n, ns))

    def omap(s, n, src, ex, ns):
        return (src[s], col(s, n, ns))

    vmem = sb_rows * k * 2 + 2 * (2 * k * bn * 4 + sb_rows * bn * 2) + 2 * k * bn * 2 + 4 * sub * bn * 4
    return pl.pallas_call(
        functools.partial(_gu_body, n_sub=n_sub, sub=sub),
        grid_spec=pltpu.PrefetchScalarGridSpec(
            num_scalar_prefetch=3,
            grid=(n_sb, n_n),
            in_specs=[pl.BlockSpec((sb_rows, k), xmap, pipeline_mode=pl.Buffered(1)),
                      pl.BlockSpec((None, k, bn), wmap),
                      pl.BlockSpec((None, k, bn), wmap)],
            out_specs=pl.BlockSpec((sb_rows, bn), omap),
            scratch_shapes=[pltpu.VMEM((k, bn), BF16), pltpu.VMEM((k, bn), BF16)]),
        out_shape=jax.ShapeDtypeStruct((rows, f), BF16),
        compiler_params=_cparams(2, vmem),
        name="ffn_gate_up",
    )(sb_src, sb_expert, sb_nsub, x, w_gate, w_up)


def _dn_body(src_ref, ex_ref, ns_ref, a_ref, w_ref, s_ref, o_ref, wb_ref, *, n_sub, sub):
    ns = ns_ref[pl.program_id(0)]
    k = pl.program_id(2)

    @pl.when(ns > 0)
    def _():
        wb_ref[...] = w_ref[...].astype(BF16)

    @pl.when(jnp.logical_and(ns > 0, k == 0))
    def _():
        o_ref[...] = jnp.zeros(o_ref.shape, F32)

    for j in range(n_sub):
        rows = slice(j * sub, (j + 1) * sub)

        @pl.when(j < ns)
        def _():
            o_ref[rows, :] += jnp.dot(a_ref[rows, :], wb_ref[...], preferred_element_type=F32)

    @pl.when(jnp.logical_and(ns > 0, k == pl.num_programs(2) - 1))
    def _():
        o_ref[...] = o_ref[...] * s_ref[...]


def _ffn_down(act, w_down, row_scale, sb_src, sb_expert, sb_nsub, sb_rows, n_sub, bn, bk):
    rows, f = act.shape
    _, _, d = w_down.shape
    n_sb = rows // sb_rows
    n_n, n_k = d // bn, f // bk
    sub = sb_rows // n_sub
    assert rows % sb_rows == 0 and d % bn == 0 and f % bk == 0 and sb_rows % n_sub == 0

    def tile(s, n, k, ns):
        unused = ns[s] == 0
        return jnp.where(unused, n_n - 1, n), jnp.where(unused, n_k - 1, k)

    def amap(s, n, k, src, ex, ns):
        n, k = tile(s, n, k, ns)
        return (src[s], k)

    def wmap(s, n, k, src, ex, ns):
        n, k = tile(s, n, k, ns)
        return (ex[s], k, n)

    def smap(s, n, k, src, ex, ns):
        return (src[s], 0)

    def omap(s, n, k, src, ex, ns):
        n, k = tile(s, n, k, ns)
        return (src[s], n)

    vmem = (2 * (sb_rows * bk * 2 + bk * bn * 4 + sb_rows * bn * 4 + sb_rows * LANES * 4) + bk * bn * 2
            + 2 * sub * bn * 4)
    return pl.pallas_call(
        functools.partial(_dn_body, n_sub=n_sub, sub=sub),
        grid_spec=pltpu.PrefetchScalarGridSpec(
            num_scalar_prefetch=3,
            grid=(n_sb, n_n, n_k),
            in_specs=[pl.BlockSpec((sb_rows, bk), amap),
                      pl.BlockSpec((None, bk, bn), wmap),
                      pl.BlockSpec((sb_rows, 1), smap)],
            out_specs=pl.BlockSpec((sb_rows, bn), omap),
            scratch_shapes=[pltpu.VMEM((bk, bn), BF16)]),
        out_shape=jax.ShapeDtypeStruct((rows, d), F32),
        compiler_params=_cparams(3, vmem),
        name="ffn_down",
    )(sb_src, sb_expert, sb_nsub, act, w_down, row_scale)


def _qk_post_body(*refs, n_q, n_kv, rope, forget):
    it = iter(refs)
    qkv_ref = next(it)
    cos_ref = sin_ref = f_ref = bf_ref = lf_ref = None
    if rope:
        cos_ref, sin_ref = next(it), next(it)
    gq_ref, gk_ref = next(it), next(it)
    if forget:
        f_ref, bf_ref = next(it), next(it)
    q_ref, k_ref, v_ref = next(it), next(it), next(it)
    if forget:
        lf_ref = next(it)
    hd = HEAD_DIM
    for h in range(n_q + n_kv):
        t = qkv_ref[:, h * hd:(h + 1) * hd]
        y = _rms(t, gq_ref[...] if h < n_q else gk_ref[...])
        if rope:
            y = y * cos_ref[...] + pltpu.roll(y, hd // 2, 1) * sin_ref[...]
        if h < n_q:
            q_ref[:, h * hd:(h + 1) * hd] = y.astype(q_ref.dtype)
        else:
            k_ref[:, (h - n_q) * hd:(h - n_q + 1) * hd] = y
    v_ref[...] = qkv_ref[:, (n_q + n_kv) * hd:(n_q + 2 * n_kv) * hd]
    if forget:
        z = f_ref[...] + bf_ref[...]
        lf_ref[...] = jnp.minimum(z, 0.0) - jnp.log(1.0 + jnp.exp(-jnp.abs(z)))


def _qk_post(qkv, gq, gk, n_q, n_kv, cos=None, sin=None, fcols=None, b_f=None):
    mp = qkv.shape[0]
    hd = HEAD_DIM
    t = _row_tile(mp)
    rope = cos is not None
    forget = fcols is not None
    row = lambda w: pl.BlockSpec((t, w), lambda i: (i, 0))
    const = lambda w: pl.BlockSpec((1, w), lambda i: (0, 0))
    args, specs = [qkv], [row(qkv.shape[1])]
    if rope:
        args += [cos, sin]
        specs += [row(hd), row(hd)]
    args += [gq.reshape(1, hd), gk.reshape(1, hd)]
    specs += [const(hd), const(hd)]
    if forget:
        args += [fcols, b_f]
        specs += [row(LANES), const(LANES)]
    out_shape = [jax.ShapeDtypeStruct((mp, n_q * hd), BF16), jax.ShapeDtypeStruct((mp, n_kv * hd), F32),
                 jax.ShapeDtypeStruct((mp, n_kv * hd), F32)]
    out_specs = [row(n_q * hd), row(n_kv * hd), row(n_kv * hd)]
    if forget:
        out_shape.append(jax.ShapeDtypeStruct((mp, LANES), F32))
        out_specs.append(row(LANES))
    vmem = 2 * t * (qkv.shape[1] * 4 + n_q * hd * 2 + 2 * n_kv * hd * 4 + 4 * LANES * 4)
    return pl.pallas_call(
        functools.partial(_qk_post_body, n_q=n_q, n_kv=n_kv, rope=rope, forget=forget),
        grid=(mp // t,),
        in_specs=specs,
        out_specs=out_specs,
        out_shape=out_shape,
        compiler_params=_cparams(1, vmem),
        name="qk_post",
    )(*args)


def _attn_prompt_body(*refs, mode, seq, tq, batch):
    o_ref = refs[-1]
    b = pl.program_id(0)

    @pl.when(b < batch)
    def _():
        _attn_prompt_compute(*refs, mode=mode, seq=seq, tq=tq)

    @pl.when(b >= batch)
    def _():
        o_ref[...] = jnp.zeros(o_ref.shape, o_ref.dtype)


def _attn_prompt_compute(*refs, mode, seq, tq):
    if mode == "fox":
        q_ref, k_ref, v_ref, cq_ref, ck_ref, o_ref = refs
    else:
        q_ref, k_ref, v_ref, o_ref = refs
    hd = HEAD_DIM
    nblk = seq // tq
    kf = k_ref[...]
    kb = kf.astype(BF16)
    vb = v_ref[...].astype(BF16)
    row = lax.broadcasted_iota(I32, (tq, tq), 0)
    col = lax.broadcasted_iota(I32, (tq, tq), 1)
    causal = col <= row
    if mode == "moba":
        kmean = jnp.mean(kf.reshape(nblk, tq, hd), axis=1)
        kmean = jnp.concatenate([kmean, jnp.zeros((LANES - nblk, hd), F32)], axis=0)
        lane = lax.broadcasted_iota(I32, (tq, LANES), 1)
        eb = lax.broadcasted_iota(I32, (LANES, seq), 0)
        ec = lax.broadcasted_iota(I32, (LANES, seq), 1)
        expand = jnp.where(jnp.logical_and(ec >= eb * tq, ec < (eb + 1) * tq), 1.0, 0.0).astype(BF16)
    for i in range(nblk):
        n_past = i * tq
        n = n_past + tq
        for hh in range(GROUP):
            qh = q_ref[i * tq:(i + 1) * tq, hh * hd:(hh + 1) * hd]
            s = lax.dot_general(qh, kb[:n], (((1,), (1,)), ((), ())), preferred_element_type=F32) * ATTN_SCALE
            if mode == "fox":
                s = s + (cq_ref[i * tq:(i + 1) * tq, hh:hh + 1] - ck_ref[hh:hh + 1, :n])
            s_diag = jnp.where(causal, s[:, n_past:], NEG_INF)
            if i == 0:
                s = s_diag
            else:
                s_past = s[:, :n_past]
                if mode == "moba" and i > MOBA_TOPK:
                    gate = lax.dot_general(qh.astype(F32), kmean, (((1,), (1,)), ((), ())),
                                           preferred_element_type=F32, precision=lax.Precision.HIGHEST)
                    gm = jnp.where(lane < i, gate, NEG_INF)
                    rank = jnp.zeros((tq, LANES), I32)
                    for d in range(1, i):
                        lower = pltpu.roll(gm, d, 1)
                        higher = pltpu.roll(gm, LANES - d, 1)
                        rank = rank + (lower >= gm).astype(I32) + (higher > gm).astype(I32)
                    sel = jnp.logical_and(rank < MOBA_TOPK, lane < i)
                    picked = jnp.dot(jnp.where(sel, 1.0, 0.0).astype(BF16), expand[:, :n_past],
                                     preferred_element_type=F32)
                    s_past = jnp.where(picked > 0.5, s_past, NEG_INF)
                s = jnp.concatenate([s_past, s_diag], axis=1)
            m = jnp.max(s, axis=1, keepdims=True)
            p = jnp.exp(s - m)
            l = jnp.sum(p, axis=1, keepdims=True)
            o = jnp.dot(p.astype(BF16), vb[:n], preferred_element_type=F32) / l
            o_ref[i * tq:(i + 1) * tq, hh * hd:(hh + 1) * hd] = o.astype(o_ref.dtype)


def _attn_prompt(q, k, v, batch, seq, mode, cq=None, ck=None):
    mp = q.shape[0]
    hd = HEAD_DIM
    n_kv = k.shape[1] // hd
    tq = MOBA_BLOCK
    assert mp > batch * seq and mp - batch * seq <= seq
    src = lambda b, g: (jnp.minimum(b, batch - 1), g)
    src4 = lambda b, g: (jnp.minimum(b, batch - 1), g, 0, 0)
    in_specs = [pl.BlockSpec((seq, GROUP * hd), src),
                pl.BlockSpec((seq, hd), src),
                pl.BlockSpec((seq, hd), src)]
    args = [q, k, v]
    if mode == "fox":
        in_specs += [pl.BlockSpec((None, None, seq, GROUP), src4),
                     pl.BlockSpec((None, None, GROUP, seq), src4)]
        args += [cq, ck]
    vmem = 2 * (2 * seq * GROUP * hd * 2 + 2 * seq * hd * 4 + seq * LANES * 4 + 8 * seq * 4) + 20 * tq * seq * 4
    return pl.pallas_call(
        functools.partial(_attn_prompt_body, mode=mode, seq=seq, tq=tq, batch=batch),
        grid=(batch + 1, n_kv),
        in_specs=in_specs,
        out_specs=pl.BlockSpec((seq, GROUP * hd), lambda b, g: (b, g)),
        out_shape=jax.ShapeDtypeStruct((mp, q.shape[1]), BF16),
        compiler_params=_cparams(2, vmem),
        name=f"attn_prompt_{mode}",
    )(*args)


def _cumsum_body(x_ref, o_ref):
    rows, seq = x_ref.shape
    r = lax.broadcasted_iota(I32, (LANES, LANES), 0)
    c = lax.broadcasted_iota(I32, (LANES, LANES), 1)
    tri = jnp.where(r <= c, 1.0, 0.0).astype(BF16)
    ones = jnp.ones((LANES, LANES), BF16)
    carry = jnp.zeros((rows, LANES), F32)
    for ch in range(seq // LANES):
        parts = _split3(x_ref[:, ch * LANES:(ch + 1) * LANES])
        o_ref[:, ch * LANES:(ch + 1) * LANES] = carry + _dot3(parts, tri)
        carry = carry + _dot3(parts, ones)


def _cumsum_lanes(x):
    b, r, l = x.shape
    spec = pl.BlockSpec((None, r, l), lambda i: (i, 0, 0))
    return pl.pallas_call(
        _cumsum_body, grid=(b,), in_specs=[spec], out_specs=spec,
        out_shape=jax.ShapeDtypeStruct(x.shape, F32),
        compiler_params=_cparams(1, 4 * r * l * 4),
        name="cumsum_lanes",
    )(x)


def _kmean_body(pt_ref, *refs, pages, per_block, n_kv):
    page_refs, o_ref = refs[:pages], refs[pages]
    keys = page_refs[0].shape[0] // n_kv
    for j in range(pages // per_block):
        tot = jnp.sum(page_refs[j * per_block][...].reshape(keys, n_kv, HEAD_DIM), axis=0)
        for t in range(1, per_block):
            tot = tot + jnp.sum(page_refs[j * per_block + t][...].reshape(keys, n_kv, HEAD_DIM), axis=0)
        o_ref[j] = tot * (1.0 / (per_block * keys))


def _cache_block_means(cache_rows, page_table, page, n_kv):
    db, n_pages = page_table.shape
    hd = cache_rows.shape[1]
    per_block = MOBA_BLOCK // page
    pages = PAGES_PER_STEP
    assert n_pages % pages == 0 and pages % per_block == 0
    blocks = pages // per_block

    def pmap(j):
        return lambda b, c, pt: (pt[b, c * pages + j], 0)

    out = pl.pallas_call(
        functools.partial(_kmean_body, pages=pages, per_block=per_block, n_kv=n_kv),
        grid_spec=pltpu.PrefetchScalarGridSpec(
            num_scalar_prefetch=1,
            grid=(db, n_pages // pages),
            in_specs=[pl.BlockSpec((page * n_kv, hd), pmap(j)) for j in range(pages)],
            out_specs=pl.BlockSpec((None, None, blocks, n_kv, hd), lambda b, c, pt: (b, c, 0, 0, 0))),
        out_shape=jax.ShapeDtypeStruct((db, n_pages // pages, blocks, n_kv, hd), F32),
        compiler_params=_cparams(2, 3 * pages * page * n_kv * hd * 4),
        name="cache_block_means",
    )(page_table, *([cache_rows] * pages))
    return out.reshape(db, n_pages // per_block, n_kv, hd)


def _moba_select_body(q_ref, km_ref, o_ref, *, n_kv):
    q = q_ref[...]
    n_heads = q.shape[0]
    nb = km_ref.shape[1]
    rowg = lax.broadcasted_iota(I32, (n_heads, nb), 0) // GROUP
    gate = jnp.zeros((n_heads, nb), F32)
    for g in range(n_kv):
        gg = lax.dot_general(q, km_ref[g], (((1,), (1,)), ((), ())),
                             preferred_element_type=F32, precision=lax.Precision.HIGHEST)
        gate = jnp.where(rowg == g, gg, gate)
    lane = lax.broadcasted_iota(I32, (n_heads, nb), 1)
    olane = lax.broadcasted_iota(I32, (n_heads, LANES), 1)
    out = jnp.zeros((n_heads, LANES), I32)
    for j in range(MOBA_TOPK):
        m = jnp.max(gate, axis=1, keepdims=True)
        idx = jnp.min(jnp.where(gate == m, lane, nb), axis=1, keepdims=True)
        out = jnp.where(olane == j, idx, out)
        gate = jnp.where(lane == idx, NEG_INF, gate)
    o_ref[...] = out


def _moba_select(q_s, kmean):
    db, n_heads, hd = q_s.shape
    n_kv, nb = kmean.shape[1:3]
    out = pl.pallas_call(
        functools.partial(_moba_select_body, n_kv=n_kv),
        grid=(db,),
        in_specs=[pl.BlockSpec((None, n_heads, hd), lambda b: (b, 0, 0)),
                  pl.BlockSpec((None, n_kv, nb, hd), lambda b: (b, 0, 0, 0))],
        out_specs=pl.BlockSpec((None, n_heads, LANES), lambda b: (b, 0, 0)),
        out_shape=jax.ShapeDtypeStruct((db, n_heads, LANES), I32),
        compiler_params=_cparams(1, 4 * n_kv * nb * hd * 4),
        name="moba_select",
    )(q_s, kmean)
    return out[:, :, :MOBA_TOPK]


def _moba_sample_body(pt_ref, sel_ref, q_ref, kn_ref, vn_ref, *refs, n_pages, n_kv):
    k_refs, v_refs, o_ref = refs[:n_pages], refs[n_pages:2 * n_pages], refs[2 * n_pages]
    q = q_ref[...]
    q8 = jnp.broadcast_to(q, (8, HEAD_DIM)).astype(BF16)
    g = pl.program_id(1) // GROUP
    lane = lax.broadcasted_iota(I32, (1, k_refs[0].shape[0]), 1)
    mine = lane % n_kv == g
    scores = [jnp.where(mine, lax.dot_general(q8, kr[...].astype(BF16), (((1,), (1,)), ((), ())),
                                              preferred_element_type=F32)[0:1] * ATTN_SCALE, NEG_INF)
              for kr in k_refs]
    s_own = jnp.sum(q * kn_ref[...], axis=1, keepdims=True) * ATTN_SCALE
    m = s_own
    for s in scores:
        m = jnp.maximum(m, jnp.max(s, axis=1, keepdims=True))
    p_own = jnp.exp(s_own - m)
    l = p_own
    acc = p_own * vn_ref[...]
    for s, vr in zip(scores, v_refs):
        p = jnp.exp(s - m)
        l = l + jnp.sum(p, axis=1, keepdims=True)
        p8 = jnp.broadcast_to(p, (8, p.shape[1])).astype(BF16)
        acc = acc + jnp.dot(p8, vr[...].astype(BF16), preferred_element_type=F32)[0:1]
    o_ref[...] = acc / l


def _moba_sample_attend(q_s, k_new, v_new, ck_rows, cv_rows, page_table, sel, page, n_kv):
    db, n_heads, hd = q_s.shape
    per_block = MOBA_BLOCK // page
    n_pages = MOBA_TOPK * per_block

    def pmap(j):
        blk, t = j // per_block, j % per_block
        return lambda b, h, pt, sl: (pt[b, sl[b, h * MOBA_TOPK + blk] * per_block + t], 0)

    one = lambda b, h, pt, sl: (b, h, 0, 0)
    kv_one = lambda b, h, pt, sl: (b, h // GROUP, 0, 0)
    row = (None, None, 1, hd)
    page_spec = [pl.BlockSpec((page * n_kv, hd), pmap(j)) for j in range(n_pages)]
    out = pl.pallas_call(
        functools.partial(_moba_sample_body, n_pages=n_pages, n_kv=n_kv),
        grid_spec=pltpu.PrefetchScalarGridSpec(
            num_scalar_prefetch=2,
            grid=(db, n_heads),
            in_specs=[pl.BlockSpec(row, one), pl.BlockSpec(row, kv_one), pl.BlockSpec(row, kv_one)]
            + page_spec + page_spec,
            out_specs=pl.BlockSpec(row, one)),
        out_shape=jax.ShapeDtypeStruct((db, n_heads, 1, hd), F32),
        compiler_params=_cparams(2, 3 * 2 * n_pages * page * n_kv * hd * 4),
        name="moba_sample_attend",
    )(page_table, sel.reshape(db, n_heads * MOBA_TOPK), q_s.reshape(db, n_heads, 1, hd),
      k_new.reshape(db, n_kv, 1, hd), v_new.reshape(db, n_kv, 1, hd),
      *([ck_rows] * n_pages), *([cv_rows] * n_pages))
    return out.reshape(db, n_heads * hd)


def _page_gather_body(pt_ref, *refs, pages):
    o_ref = refs[pages]
    rows = refs[0].shape[0]
    for j in range(pages):
        o_ref[j * rows:(j + 1) * rows, :] = refs[j][...]


def _gather_logf_pages(cache_logf, page_table):
    db, n_tab = page_table.shape
    _, page, n_heads = cache_logf.shape
    pages = 2 * PAGES_PER_STEP
    assert n_tab % pages == 0

    def pmap(j):
        return lambda b, c, pt: (pt[b, c * pages + j], 0, 0)

    return pl.pallas_call(
        functools.partial(_page_gather_body, pages=pages),
        grid_spec=pltpu.PrefetchScalarGridSpec(
            num_scalar_prefetch=1,
            grid=(db, n_tab // pages),
            in_specs=[pl.BlockSpec((None, page, n_heads), pmap(j)) for j in range(pages)],
            out_specs=pl.BlockSpec((None, pages * page, n_heads), lambda b, c, pt: (b, c, 0))),
        out_shape=jax.ShapeDtypeStruct((db, n_tab * page, n_heads), F32),
        compiler_params=_cparams(2, 4 * pages * page * LANES * 4),
        name="gather_logf_pages",
    )(page_table, *([cache_logf] * pages))


def _suffix_body(x_ref, init_ref, o_ref, r_ref):
    c = pl.program_id(1)

    @pl.when(c == 0)
    def _():
        r_ref[...] = init_ref[...]

    rr = lax.broadcasted_iota(I32, (LANES, LANES), 0)
    cc = lax.broadcasted_iota(I32, (LANES, LANES), 1)
    later = jnp.where(rr > cc, 1.0, 0.0).astype(BF16)
    ones = jnp.ones((LANES, LANES), BF16)
    for j in range(x_ref.shape[1] // LANES - 1, -1, -1):
        parts = _split3(x_ref[:, j * LANES:(j + 1) * LANES])
        r = r_ref[...]
        o_ref[:, j * LANES:(j + 1) * LANES] = r + _dot3(parts, later)
        r_ref[...] = r + _dot3(parts, ones)


def _suffix_sum_lanes(x, init, chunk=2048):
    b, r, l = x.shape
    n_chunks = l // chunk
    assert l % chunk == 0
    rev = lambda i, c: (i, 0, n_chunks - 1 - c)
    return pl.pallas_call(
        _suffix_body,
        grid=(b, n_chunks),
        in_specs=[pl.BlockSpec((None, r, chunk), rev), pl.BlockSpec((None, r, LANES), lambda i, c: (i, 0, 0))],
        out_specs=pl.BlockSpec((None, r, chunk), rev),
        out_shape=jax.ShapeDtypeStruct(x.shape, F32),
        scratch_shapes=[pltpu.VMEM((r, LANES), F32)],
        compiler_params=_cparams(2, 8 * r * chunk * 4),
        name="suffix_sum_lanes",
    )(x, init)


def _fox_sample_body(pt_ref, q_ref, kn_ref, vn_ref, bias_ref, *refs, pages, n_kv):
    k_refs, v_refs = refs[:pages], refs[pages:2 * pages]
    o_ref, m_ref, l_ref, acc_ref = refs[2 * pages:]
    c = pl.program_id(1)
    n_heads = q_ref.shape[0]
    rows = k_refs[0].shape[0]

    @pl.when(c == 0)
    def _():
        m_ref[...] = jnp.full(m_ref.shape, NEG_INF, F32)
        l_ref[...] = jnp.zeros(l_ref.shape, F32)
        acc_ref[...] = jnp.zeros(acc_ref.shape, F32)

    q = q_ref[...]

    def absorb(s, vb):
        m_old = m_ref[...]
        m_new = jnp.maximum(m_old, jnp.max(s, axis=1, keepdims=True))
        alpha = jnp.exp(m_old - m_new)
        p = jnp.exp(s - m_new[:, 0:1])
        l_ref[...] = alpha * l_ref[...] + jnp.sum(p, axis=1, keepdims=True)
        acc_ref[...] = acc_ref[...] * alpha + jnp.dot(p.astype(BF16), vb, preferred_element_type=F32)
        m_ref[...] = m_new

    for j in range(pages):
        s = lax.dot_general(q, k_refs[j][...].astype(BF16), (((1,), (1,)), ((), ())),
                            preferred_element_type=F32) * ATTN_SCALE + bias_ref[:, j * rows:(j + 1) * rows]
        absorb(s, v_refs[j][...].astype(BF16))

    @pl.when(c == pl.num_programs(1) - 1)
    def _():
        s = lax.dot_general(q, kn_ref[...].astype(BF16), (((1,), (1,)), ((), ())),
                            preferred_element_type=F32) * ATTN_SCALE
        rowg = lax.broadcasted_iota(I32, s.shape, 0) // GROUP
        lane = lax.broadcasted_iota(I32, s.shape, 1)
        absorb(jnp.where(lane == rowg, s, NEG_INF), vn_ref[...].astype(BF16))
        o_ref[...] = acc_ref[...] / l_ref[...]


def _fox_sample_attend(q_s, k_new, v_new, bias, ck_rows, cv_rows, page_table, page, n_kv):
    db, n_heads, hd = q_s.shape
    n_tab = page_table.shape[1]
    pages = PAGES_PER_STEP
    n_chunks = n_tab // pages
    rows = page * n_kv
    assert n_tab % pages == 0 and k_new.shape[1] == LANES

    def pmap(j):
        return lambda b, c, pt: (pt[b, c * pages + j], 0)

    per_b = lambda b, c, pt: (b, 0, 0)
    page_spec = [pl.BlockSpec((rows, hd), pmap(j)) for j in range(pages)]
    out = pl.pallas_call(
        functools.partial(_fox_sample_body, pages=pages, n_kv=n_kv),
        grid_spec=pltpu.PrefetchScalarGridSpec(
            num_scalar_prefetch=1,
            grid=(db, n_chunks),
            in_specs=[pl.BlockSpec((None, n_heads, hd), per_b),
                      pl.BlockSpec((None, LANES, hd), per_b),
                      pl.BlockSpec((None, LANES, hd), per_b),
                      pl.BlockSpec((None, n_heads, pages * rows), lambda b, c, pt: (b, 0, c))]
            + page_spec + page_spec,
            out_specs=pl.BlockSpec((None, n_heads, hd), per_b),
            scratch_shapes=[pltpu.VMEM((n_heads, LANES), F32), pltpu.VMEM((n_heads, LANES), F32),
                            pltpu.VMEM((n_heads, hd), F32)]),
        out_shape=jax.ShapeDtypeStruct((db, n_heads, hd), F32),
        compiler_params=_cparams(2, 3 * 2 * pages * rows * hd * 4 + 4 * n_heads * pages * rows * 4),
        name="fox_sample_attend",
    )(page_table, q_s, k_new, v_new, bias, *([ck_rows] * pages), *([cv_rows] * pages))
    return out.reshape(db, n_heads * hd)


def _row_copy(src_hbm, row, dst, slot, sem):
    return pltpu.make_async_copy(src_hbm.at[pl.ds(row, 1)], dst.at[pl.ds(slot, 1)], sem)


def _gather_norm_body(tok_ref, used_ref, x_hbm, g_ref, o_ref, buf_ref, sem, *, rows, per_sb):
    i = pl.program_id(0)
    base = i * rows
    live = (i % per_sb) * rows < used_ref[i // per_sb]

    @pl.when(live)
    def _():
        def issue(r, carry):
            _row_copy(x_hbm, tok_ref[base + r], buf_ref, r, sem).start()
            return carry

        def wait(r, carry):
            _row_copy(x_hbm, tok_ref[base + r], buf_ref, r, sem).wait()
            return carry

        lax.fori_loop(0, rows, issue, 0)
        lax.fori_loop(0, rows, wait, 0)
        o_ref[...] = _rms(buf_ref[...], g_ref[...]).astype(o_ref.dtype)

    @pl.when(jnp.logical_not(live))
    def _():
        o_ref[...] = jnp.zeros(o_ref.shape, o_ref.dtype)


def _moe_gather_norm(x, g, slot_tok, sb_used_rows, sb_rows, rows=256):
    cap = slot_tok.shape[0]
    d = x.shape[1]
    assert cap % sb_rows == 0 and sb_rows % rows == 0
    return pl.pallas_call(
        functools.partial(_gather_norm_body, rows=rows, per_sb=sb_rows // rows),
        grid_spec=pltpu.PrefetchScalarGridSpec(
            num_scalar_prefetch=2,
            grid=(cap // rows,),
            in_specs=[pl.BlockSpec(memory_space=pl.ANY), pl.BlockSpec((1, d), lambda i, t, n: (0, 0))],
            out_specs=pl.BlockSpec((rows, d), lambda i, t, n: (i, 0)),
            scratch_shapes=[pltpu.VMEM((rows, d), F32), pltpu.SemaphoreType.DMA(())]),
        out_shape=jax.ShapeDtypeStruct((cap, d), BF16),
        compiler_params=_cparams(1, rows * d * 4 * 3 + 2 * rows * d * 2),
        name="moe_gather_norm",
    )(slot_tok, sb_used_rows, x, g.reshape(1, d))


def _combine_body(pos_ref, x_ref, y_hbm, o_ref, buf_ref, sem, *, rows, n_tok):
    i = pl.program_id(0)
    base = i * rows

    def issue(r, carry):
        for j in range(MOE_TOPK):
            _row_copy(y_hbm, pos_ref[(base + r) * MOE_TOPK + j], buf_ref.at[j], r, sem).start()
        return carry

    def wait(r, carry):
        for j in range(MOE_TOPK):
            _row_copy(y_hbm, pos_ref[(base + r) * MOE_TOPK + j], buf_ref.at[j], r, sem).wait()
        return carry

    lax.fori_loop(0, rows, issue, 0)
    lax.fori_loop(0, rows, wait, 0)
    tok = base + lax.broadcasted_iota(I32, (rows, 1), 0)
    y = buf_ref[0]
    for j in range(1, MOE_TOPK):
        y = y + buf_ref[j]
    o_ref[...] = x_ref[...] + jnp.where(tok < n_tok, y, 0.0)


def _moe_combine(x, y, pos, n_tok, rows=128):
    mp, d = x.shape
    assert mp % rows == 0
    return pl.pallas_call(
        functools.partial(_combine_body, rows=rows, n_tok=n_tok),
        grid_spec=pltpu.PrefetchScalarGridSpec(
            num_scalar_prefetch=1,
            grid=(mp // rows,),
            in_specs=[pl.BlockSpec((rows, d), lambda i, p: (i, 0)), pl.BlockSpec(memory_space=pl.ANY)],
            out_specs=pl.BlockSpec((rows, d), lambda i, p: (i, 0)),
            scratch_shapes=[pltpu.VMEM((MOE_TOPK, rows, d), F32), pltpu.SemaphoreType.DMA(())]),
        out_shape=jax.ShapeDtypeStruct((mp, d), F32),
        compiler_params=_cparams(1, (MOE_TOPK + 4) * rows * d * 4),
        name="moe_combine",
    )(pos, x, y)


def _moe_plan(top_idx, top_gate, n_experts, zero_row):
    n_tok = top_idx.shape[0]
    n_asg = n_tok * MOE_TOPK
    sbr, sub = MOE_SUPER_ROWS, MOE_SUPER_ROWS // MOE_SUB_BLOCKS
    n_sb = n_experts + n_asg // sbr
    cap = n_sb * sbr
    e_flat = top_idx.reshape(n_asg)
    onehot = (e_flat[:, None] == jnp.arange(n_experts, dtype=I32)[None, :]).astype(I32)
    csum = jnp.cumsum(onehot, axis=0)
    rank = jnp.sum(csum * onehot, axis=1) - 1
    counts = csum[-1]
    sb_per_e = (counts + sbr - 1) // sbr
    sb_end = jnp.cumsum(sb_per_e)
    sb_start = sb_end - sb_per_e
    dest = (jnp.sum((sb_start * sbr)[None, :] * onehot, axis=1) + rank).astype(I32)
    tok = jnp.arange(n_asg, dtype=I32) // MOE_TOPK
    slot_tok = jnp.full((cap,), zero_row, I32).at[dest].set(tok)
    slot_gate = jnp.zeros((cap,), F32).at[dest].set(top_gate.reshape(n_asg))
    s = jnp.arange(n_sb, dtype=I32)
    used = s < sb_end[-1]
    last = jnp.maximum(sb_end[-1] - 1, 0)
    src = jnp.where(used, s, last).astype(I32)
    expert = jnp.minimum(jnp.searchsorted(sb_end, src, side="right"), n_experts - 1).astype(I32)
    rows_in = jnp.clip(counts[expert] - (src - sb_start[expert]) * sbr, 0, sbr)
    n_sub = jnp.where(used, (rows_in + sub - 1) // sub, 0).astype(I32)
    return slot_tok, slot_gate, dest, src, expert, n_sub


def kernel(x_prompt, x_sample, cache_k0, cache_v0, cache_k1, cache_v1, cache_logf1, page_table, attn_norm, ffn_norm, w_qkv0, q_norm0, k_norm0, w_o0, w_qkvf1, b_f1, q_norm1, k_norm1, w_o1, w_gate0, w_up0, w_down0, w_router1, w_gate1, w_up1, w_down1):
    batch, seq, d_model = x_prompt.shape
    db, dec_seq, _ = x_sample.shape
    assert dec_seq == 1 and db <= SAMPLE_ROWS
    hd = HEAD_DIM
    n_heads = d_model // hd
    n_kv = cache_k0.shape[2]
    assert n_heads == GROUP * n_kv and seq % MOBA_BLOCK == 0
    page = cache_k0.shape[1]
    n_tab = page_table.shape[1]
    past_len = n_tab * page
    assert past_len % MOBA_BLOCK == 0
    n_experts = w_router1.shape[1]
    p_rows = batch * seq
    n_tok = p_rows + db
    mp = p_rows + SAMPLE_ROWS
    nq, nkv = n_heads * hd, n_kv * hd
    page_table = page_table.astype(I32)

    x0 = jnp.concatenate([x_prompt.reshape(p_rows, d_model), x_sample.reshape(db, d_model),
                          jnp.zeros((mp - n_tok, d_model), F32)], axis=0)

    pos = jnp.concatenate([jnp.tile(jnp.arange(seq, dtype=I32), batch), jnp.full((db,), past_len, I32),
                           jnp.zeros((mp - n_tok,), I32)])
    half = hd // 2
    inv_freq = ROPE_THETA ** (-jnp.arange(half, dtype=F32) * 2.0 / hd)
    ang = pos.astype(F32)[:, None] * inv_freq[None, :]
    cos = jnp.concatenate([jnp.cos(ang), jnp.cos(ang)], axis=1)
    sin = jnp.concatenate([-jnp.sin(ang), jnp.sin(ang)], axis=1)

    def place_sample_rows(o, o_s):
        blk = jnp.concatenate([o_s.astype(o.dtype), jnp.zeros((SAMPLE_ROWS - db, o.shape[1]), o.dtype)], axis=0)
        return lax.dynamic_update_slice(o, blk, (p_rows, 0))

    h = _rmsnorm(x0, attn_norm[0])
    qkv = _matmul(h, w_qkv0, nq + 2 * nkv, 512)
    q0, k0, v0 = _qk_post(qkv, q_norm0, k_norm0, n_heads, n_kv, cos=cos, sin=sin)
    o = _attn_prompt(q0, k0, v0, batch, seq, "moba")
    ck0 = cache_k0.reshape(-1, hd)
    cv0 = cache_v0.reshape(-1, hd)
    q0_s = q0[p_rows:n_tok].astype(F32).reshape(db, n_heads, hd)
    kmean = _cache_block_means(ck0, page_table, page, n_kv).transpose(0, 2, 1, 3)
    sel = _moba_select(q0_s, kmean)
    o_s = _moba_sample_attend(q0_s, k0[p_rows:n_tok], v0[p_rows:n_tok], ck0, cv0, page_table, sel, page, n_kv)
    o = place_sample_rows(o, o_s)
    a0 = _matmul(o, w_o0, d_model, 512)

    x1, h = _add_rmsnorm(x0, a0, ffn_norm[0])
    dense_sb = 4
    assert mp % (dense_sb * MOE_SUB_BLOCKS * 16) == 0
    d_src = jnp.arange(dense_sb, dtype=I32)
    d_exp = jnp.zeros((dense_sb,), I32)
    d_nsub = jnp.full((dense_sb,), MOE_SUB_BLOCKS, I32)
    act = _ffn_gate_up(h, w_gate0[None], w_up0[None], d_src, d_exp, d_nsub, mp // dense_sb, MOE_SUB_BLOCKS, 256)
    f0 = _ffn_down(act, w_down0[None], jnp.ones((mp, 1), F32), d_src, d_exp, d_nsub,
                   mp // dense_sb, MOE_SUB_BLOCKS, 1024, 1024)

    x2, h = _add_rmsnorm(x1, f0, attn_norm[1])
    qkv = _matmul(h, w_qkvf1, nq + 2 * nkv, 512)
    w_f = jnp.pad(w_qkvf1[:, nq + 2 * nkv:], ((0, 0), (0, LANES - n_heads)))
    fcols = _matmul(h, w_f, LANES, LANES)
    b_f = jnp.pad(b_f1.reshape(1, n_heads), ((0, 0), (0, LANES - n_heads)))
    q1, k1, v1, lf_pad = _qk_post(qkv, q_norm1, k_norm1, n_heads, n_kv, fcols=fcols, b_f=b_f)
    logf = lf_pad[:, :n_heads]
    lf_p = logf[:p_rows].reshape(batch, seq, n_heads)
    c_t = _cumsum_lanes(lf_p.transpose(0, 2, 1))
    ck = c_t.reshape(batch, n_kv, GROUP, seq)
    cq = ck.transpose(0, 1, 3, 2)
    o = _attn_prompt(q1, k1, v1, batch, seq, "fox", cq=cq, ck=ck)
    ck1 = cache_k1.reshape(-1, hd)
    cv1 = cache_v1.reshape(-1, hd)
    q1_s = q1[p_rows:n_tok].reshape(db, n_heads, hd)
    lf_new = jnp.broadcast_to(logf[p_rows:n_tok].reshape(db, n_heads, 1), (db, n_heads, LANES))
    lf_t = _gather_logf_pages(cache_logf1, page_table).transpose(0, 2, 1)
    fbias = _suffix_sum_lanes(lf_t, lf_new)
    own_head = (jnp.arange(n_kv, dtype=I32)[None, :] == jnp.arange(n_heads, dtype=I32)[:, None] // GROUP)
    fbias = jnp.where(own_head[None, :, None, :], fbias[:, :, :, None], NEG_INF).reshape(db, n_heads, past_len * n_kv)
    pad_new = lambda t: jnp.pad(t[p_rows:n_tok].reshape(db, n_kv, hd), ((0, 0), (0, LANES - n_kv), (0, 0)))
    o_s = _fox_sample_attend(q1_s, pad_new(k1), pad_new(v1), fbias, ck1, cv1, page_table, page, n_kv)
    o = place_sample_rows(o, o_s)
    a1 = _matmul(o, w_o1, d_model, 512)

    x3, ridx, rgate = _add_rmsnorm_router(x2, a1, ffn_norm[1], w_router1)
    slot_tok, slot_gate, dest, sb_src, sb_expert, sb_nsub = _moe_plan(
        ridx[:n_tok, :MOE_TOPK], rgate[:n_tok, :MOE_TOPK], n_experts, n_tok)
    sub_rows = MOE_SUPER_ROWS // MOE_SUB_BLOCKS
    xs = _moe_gather_norm(x3, ffn_norm[1], slot_tok, sb_nsub * sub_rows, MOE_SUPER_ROWS)
    act = _ffn_gate_up(xs, w_gate1, w_up1, sb_src, sb_expert, sb_nsub, MOE_SUPER_ROWS, MOE_SUB_BLOCKS, 256)
    ys = _ffn_down(act, w_down1, slot_gate[:, None], sb_src, sb_expert, sb_nsub,
                   MOE_SUPER_ROWS, MOE_SUB_BLOCKS, 1024, 1024)
    pos_pad = jnp.concatenate([dest, jnp.zeros(((mp - n_tok) * MOE_TOPK,), I32)])
    y = _moe_combine(x3, ys, pos_pad, n_tok)

    def split(t, tail):
        return t[:p_rows].reshape((batch, seq) + tail), t[p_rows:n_tok].reshape((db, 1) + tail)

    y_p, y_s = split(y, (d_model,))
    k0_p, k0_s = split(k0, (n_kv, hd))
    v0_p, v0_s = split(v0, (n_kv, hd))
    k1_p, k1_s = split(k1, (n_kv, hd))
    v1_p, v1_s = split(v1, (n_kv, hd))
    lf_p4, lf_s = split(logf, (n_heads,))
    return (y_p, y_s, k0_p, v0_p, k1_p, v1_p, lf_p4, k0_s, v0_s, k1_s, v1_s, lf_s)
```

```python
import functools

import jax
import jax.numpy as jnp
from jax import lax
from jax.experimental import pallas as pl
from jax.experimental.pallas import tpu as pltpu

F32 = jnp.float32
BF16 = jnp.bfloat16
I32 = jnp.int32

HEAD_DIM = 128
GROUP = 4
MOBA_BLOCK = 256
MOBA_TOPK = 3
MOE_TOPK = 2
ROPE_THETA = 10000.0
EPS = 1e-6
ATTN_SCALE = HEAD_DIM ** -0.5

LANES = 128
V7X_VMEM_BYTES = 64 * 1024 * 1024
COMPILER_SCRATCH_BYTES = 6 * 1024 * 1024
NEG_INF = float("-inf")

SAMPLE_ROWS = 128
MOE_SUPER_ROWS = 2560
MOE_SUB_BLOCKS = 5
PAGES_PER_STEP = 8


def _cparams(n_axes, vmem_bytes):
    limit = min(int(vmem_bytes) + COMPILER_SCRATCH_BYTES, V7X_VMEM_BYTES - 2 * 1024 * 1024)
    return pltpu.CompilerParams(
        dimension_semantics=("arbitrary",) * n_axes, vmem_limit_bytes=limit)


def _rms(x, g):
    ms = jnp.mean(x * x, axis=-1, keepdims=True)
    return x * lax.rsqrt(ms + EPS) * g


def _split3(x):
    a1 = x.astype(BF16)
    r1 = x - a1.astype(F32)
    a2 = r1.astype(BF16)
    r2 = r1 - a2.astype(F32)
    return a1, a2, r2.astype(BF16)


def _dot3(parts, m):
    acc = jnp.dot(parts[0], m, preferred_element_type=F32)
    acc = acc + jnp.dot(parts[1], m, preferred_element_type=F32)
    return acc + jnp.dot(parts[2], m, preferred_element_type=F32)


def _rms_body(x_ref, g_ref, h_ref):
    h_ref[...] = _rms(x_ref[...], g_ref[...]).astype(h_ref.dtype)


def _add_rms_body(x_ref, d_ref, g_ref, xo_ref, h_ref):
    x = x_ref[...] + d_ref[...]
    xo_ref[...] = x
    h_ref[...] = _rms(x, g_ref[...]).astype(h_ref.dtype)


def _add_rms_router_body(x_ref, d_ref, g_ref, wr_ref, xo_ref, idx_ref, gate_ref, *, n_experts):
    x = x_ref[...] + d_ref[...]
    xo_ref[...] = x
    h = _rms(x, g_ref[...])
    logits = jnp.dot(h, wr_ref[...], preferred_element_type=F32, precision=lax.Precision.HIGHEST)
    lane = lax.broadcasted_iota(I32, logits.shape, 1)
    s = jnp.where(lane < n_experts, logits, NEG_INF)
    m1 = jnp.max(s, axis=1, keepdims=True)
    i1 = jnp.min(jnp.where(s == m1, lane, LANES), axis=1, keepdims=True)
    s2 = jnp.where(lane == i1, NEG_INF, s)
    m2 = jnp.max(s2, axis=1, keepdims=True)
    i2 = jnp.min(jnp.where(s2 == m2, lane, LANES), axis=1, keepdims=True)
    e = jnp.exp(m2 - m1)
    g1 = 1.0 / (1.0 + e)
    g2 = e / (1.0 + e)
    idx_ref[...] = jnp.where(lane == 0, i1, jnp.where(lane == 1, i2, 0))
    gate_ref[...] = jnp.where(lane == 0, g1, jnp.where(lane == 1, g2, 0.0))


def _row_tile(mp):
    for t in (320, 256, 128):
        if mp % t == 0:
            return t
    raise ValueError(f"unsupported row count {mp}")


def _rmsnorm(x, g):
    mp, d = x.shape
    t = _row_tile(mp)
    return pl.pallas_call(
        _rms_body,
        grid=(mp // t,),
        in_specs=[pl.BlockSpec((t, d), lambda i: (i, 0)), pl.BlockSpec((1, d), lambda i: (0, 0))],
        out_specs=pl.BlockSpec((t, d), lambda i: (i, 0)),
        out_shape=jax.ShapeDtypeStruct((mp, d), BF16),
        compiler_params=_cparams(1, 2 * t * d * 6),
        name="rmsnorm",
    )(x, g.reshape(1, d))


def _add_rmsnorm(x, delta, g):
    mp, d = x.shape
    t = _row_tile(mp)
    row = pl.BlockSpec((t, d), lambda i: (i, 0))
    return pl.pallas_call(
        _add_rms_body,
        grid=(mp // t,),
        in_specs=[row, row, pl.BlockSpec((1, d), lambda i: (0, 0))],
        out_specs=[row, row],
        out_shape=[jax.ShapeDtypeStruct((mp, d), F32), jax.ShapeDtypeStruct((mp, d), BF16)],
        compiler_params=_cparams(1, 2 * t * d * 14),
        name="add_rmsnorm",
    )(x, delta, g.reshape(1, d))


def _add_rmsnorm_router(x, delta, g, w_router):
    mp, d = x.shape
    n_experts = w_router.shape[1]
    t = _row_tile(mp)
    wr = jnp.pad(w_router, ((0, 0), (0, LANES - n_experts)))
    row = pl.BlockSpec((t, d), lambda i: (i, 0))
    small = pl.BlockSpec((t, LANES), lambda i: (i, 0))
    return pl.pallas_call(
        functools.partial(_add_rms_router_body, n_experts=n_experts),
        grid=(mp // t,),
        in_specs=[row, row, pl.BlockSpec((1, d), lambda i: (0, 0)), pl.BlockSpec((d, LANES), lambda i: (0, 0))],
        out_specs=[row, small, small],
        out_shape=[jax.ShapeDtypeStruct((mp, d), F32), jax.ShapeDtypeStruct((mp, LANES), I32),
                   jax.ShapeDtypeStruct((mp, LANES), F32)],
        compiler_params=_cparams(1, 2 * t * d * 12 + 2 * d * LANES * 4 + 8 * t * d),
        name="add_rmsnorm_router",
    )(x, delta, g.reshape(1, d), wr)


def _mm_body(x_ref, w_ref, o_ref, wbf_ref):
    wbf_ref[...] = w_ref[...].astype(BF16)
    o_ref[...] = jnp.dot(x_ref[...], wbf_ref[...], preferred_element_type=F32).astype(o_ref.dtype)


def _matmul(x, w, n_out, bn):
    mp, k = x.shape
    bm = mp // 8
    assert mp % bm == 0 and n_out % bn == 0
    vmem = 2 * (bm * k * 2 + k * bn * 4 + bm * bn * 4) + k * bn * 2
    return pl.pallas_call(
        _mm_body,
        grid=(mp // bm, n_out // bn),
        in_specs=[pl.BlockSpec((bm, k), lambda m, n: (m, 0)),
                  pl.BlockSpec((k, bn), lambda m, n: (0, n))],
        out_specs=pl.BlockSpec((bm, bn), lambda m, n: (m, n)),
        out_shape=jax.ShapeDtypeStruct((mp, n_out), F32),
        scratch_shapes=[pltpu.VMEM((k, bn), BF16)],
        compiler_params=_cparams(2, vmem),
        name="matmul",
    )(x, w)


def _sub_block_cases(units, j):
    n_full = units // 2
    odd = units % 2
    return j < n_full, jnp.logical_and(j == n_full, odd == 1), jnp.logical_and(j >= n_full + odd, units > 0)


def _gu_body(src_ref, ex_ref, un_ref, x_ref, wg_ref, wu_ref, o_ref, wgb_ref, wub_ref, *, n_sub, sub):
    units = un_ref[pl.program_id(0)]
    half = sub // 2

    @pl.when(units > 0)
    def _():
        wgb_ref[...] = wg_ref[...].astype(BF16)
        wub_ref[...] = wu_ref[...].astype(BF16)

    def swiglu(rows):
        x = x_ref[rows, :]
        g = jnp.dot(x, wgb_ref[...], preferred_element_type=F32)
        u = jnp.dot(x, wub_ref[...], preferred_element_type=F32)
        o_ref[rows, :] = (g * jax.nn.sigmoid(g) * u).astype(o_ref.dtype)

    def clear(rows):
        o_ref[rows, :] = jnp.zeros((rows.stop - rows.start, o_ref.shape[1]), o_ref.dtype)

    for j in range(n_sub):
        full, first_half, unused = _sub_block_cases(units, j)

        @pl.when(full)
        def _():
            swiglu(slice(j * sub, (j + 1) * sub))

        @pl.when(first_half)
        def _():
            swiglu(slice(j * sub, j * sub + half))
            clear(slice(j * sub + half, (j + 1) * sub))

        @pl.when(unused)
        def _():
            clear(slice(j * sub, (j + 1) * sub))


def _ffn_gate_up(x, w_gate, w_up, sb_src, sb_expert, sb_nsub, sb_rows, n_sub, bn):
    rows, k = x.shape
    _, _, f = w_gate.shape
    n_sb = rows // sb_rows
    n_n = f // bn
    sub = sb_rows // n_sub
    assert rows % sb_rows == 0 and f % bn == 0 and sb_rows % n_sub == 0

    def col(s, n, ns):
        return jnp.where(ns[s] == 0, n_n - 1, n)

    def xmap(s, n, src, ex, ns):
        return (src[s], 0)

    def wmap(s, n, src, ex, ns):
        return (ex[s], 0, col(s, n, ns))

    def omap(s, n, src, ex, ns):
        return (src[s], col(s, n, ns))

    vmem = sb_rows * k * 2 + 2 * (2 * k * bn * 4 + sb_rows * bn * 2) + 2 * k * bn * 2 + 4 * sub * bn * 4
    return pl.pallas_call(
        functools.partial(_gu_body, n_sub=n_sub, sub=sub),
        grid_spec=pltpu.PrefetchScalarGridSpec(
            num_scalar_prefetch=3,
            grid=(n_sb, n_n),
            in_specs=[pl.BlockSpec((sb_rows, k), xmap, pipeline_mode=pl.Buffered(1)),
                      pl.BlockSpec((None, k, bn), wmap),
                      pl.BlockSpec((None, k, bn), wmap)],
            out_specs=pl.BlockSpec((sb_rows, bn), omap),
            scratch_shapes=[pltpu.VMEM((k, bn), BF16), pltpu.VMEM((k, bn), BF16)]),
        out_shape=jax.ShapeDtypeStruct((rows, f), BF16),
        compiler_params=_cparams(2, vmem),
        name="ffn_gate_up",
    )(sb_src, sb_expert, sb_nsub, x, w_gate, w_up)


def _dn_body(src_ref, ex_ref, un_ref, a_ref, w_ref, s_ref, o_ref, wb_ref, *, n_sub, sub):
    units = un_ref[pl.program_id(0)]
    k = pl.program_id(2)

    @pl.when(units > 0)
    def _():
        wb_ref[...] = w_ref[...].astype(BF16)

    @pl.when(jnp.logical_and(units > 0, k == 0))
    def _():
        o_ref[...] = jnp.zeros(o_ref.shape, F32)

    def accumulate(rows):
        o_ref[rows, :] += jnp.dot(a_ref[rows, :], wb_ref[...], preferred_element_type=F32)

    for j in range(n_sub):
        full, first_half, _ = _sub_block_cases(units, j)

        @pl.when(full)
        def _():
            accumulate(slice(j * sub, (j + 1) * sub))

        @pl.when(first_half)
        def _():
            accumulate(slice(j * sub, j * sub + sub // 2))

    @pl.when(jnp.logical_and(units > 0, k == pl.num_programs(2) - 1))
    def _():
        o_ref[...] = o_ref[...] * s_ref[...]


def _ffn_down(act, w_down, row_scale, sb_src, sb_expert, sb_nsub, sb_rows, n_sub, bn, bk):
    rows, f = act.shape
    _, _, d = w_down.shape
    n_sb = rows // sb_rows
    n_n, n_k = d // bn, f // bk
    sub = sb_rows // n_sub
    assert rows % sb_rows == 0 and d % bn == 0 and f % bk == 0 and sb_rows % n_sub == 0

    def tile(s, n, k, ns):
        unused = ns[s] == 0
        return jnp.where(unused, n_n - 1, n), jnp.where(unused, n_k - 1, k)

    def amap(s, n, k, src, ex, ns):
        n, k = tile(s, n, k, ns)
        return (src[s], k)

    def wmap(s, n, k, src, ex, ns):
        n, k = tile(s, n, k, ns)
        return (ex[s], k, n)

    def smap(s, n, k, src, ex, ns):
        return (src[s], 0)

    def omap(s, n, k, src, ex, ns):
        n, k = tile(s, n, k, ns)
        return (src[s], n)

    vmem = (2 * (sb_rows * bk * 2 + bk * bn * 4 + sb_rows * bn * 4 + sb_rows * LANES * 4) + bk * bn * 2
            + 2 * sub * bn * 4)
    return pl.pallas_call(
        functools.partial(_dn_body, n_sub=n_sub, sub=sub),
        grid_spec=pltpu.PrefetchScalarGridSpec(
            num_scalar_prefetch=3,
            grid=(n_sb, n_n, n_k),
            in_specs=[pl.BlockSpec((sb_rows, bk), amap),
                      pl.BlockSpec((None, bk, bn), wmap),
                      pl.BlockSpec((sb_rows, 1), smap)],
            out_specs=pl.BlockSpec((sb_rows, bn), omap),
            scratch_shapes=[pltpu.VMEM((bk, bn), BF16)]),
        out_shape=jax.ShapeDtypeStruct((rows, d), F32),
        compiler_params=_cparams(3, vmem),
        name="ffn_down",
    )(sb_src, sb_expert, sb_nsub, act, w_down, row_scale)


def _qk_post_body(*refs, n_q, n_kv, rope, forget):
    it = iter(refs)
    qkv_ref = next(it)
    cos_ref = sin_ref = f_ref = bf_ref = lf_ref = None
    if rope:
        cos_ref, sin_ref = next(it), next(it)
    gq_ref, gk_ref = next(it), next(it)
    if forget:
        f_ref, bf_ref = next(it), next(it)
    q_ref, k_ref, v_ref = next(it), next(it), next(it)
    if forget:
        lf_ref = next(it)
    hd = HEAD_DIM
    for h in range(n_q + n_kv):
        t = qkv_ref[:, h * hd:(h + 1) * hd]
        y = _rms(t, gq_ref[...] if h < n_q else gk_ref[...])
        if rope:
            y = y * cos_ref[...] + pltpu.roll(y, hd // 2, 1) * sin_ref[...]
        if h < n_q:
            q_ref[:, h * hd:(h + 1) * hd] = y.astype(q_ref.dtype)
        else:
            k_ref[:, (h - n_q) * hd:(h - n_q + 1) * hd] = y
    v_ref[...] = qkv_ref[:, (n_q + n_kv) * hd:(n_q + 2 * n_kv) * hd]
    if forget:
        z = f_ref[...] + bf_ref[...]
        lf_ref[...] = jnp.minimum(z, 0.0) - jnp.log(1.0 + jnp.exp(-jnp.abs(z)))


def _qk_post(qkv, gq, gk, n_q, n_kv, cos=None, sin=None, fcols=None, b_f=None):
    mp = qkv.shape[0]
    hd = HEAD_DIM
    t = _row_tile(mp)
    rope = cos is not None
    forget = fcols is not None
    row = lambda w: pl.BlockSpec((t, w), lambda i: (i, 0))
    const = lambda w: pl.BlockSpec((1, w), lambda i: (0, 0))
    args, specs = [qkv], [row(qkv.shape[1])]
    if rope:
        args += [cos, sin]
        specs += [row(hd), row(hd)]
    args += [gq.reshape(1, hd), gk.reshape(1, hd)]
    specs += [const(hd), const(hd)]
    if forget:
        args += [fcols, b_f]
        specs += [row(LANES), const(LANES)]
    out_shape = [jax.ShapeDtypeStruct((mp, n_q * hd), BF16), jax.ShapeDtypeStruct((mp, n_kv * hd), F32),
                 jax.ShapeDtypeStruct((mp, n_kv * hd), F32)]
    out_specs = [row(n_q * hd), row(n_kv * hd), row(n_kv * hd)]
    if forget:
        out_shape.append(jax.ShapeDtypeStruct((mp, LANES), F32))
        out_specs.append(row(LANES))
    vmem = 2 * t * (qkv.shape[1] * 4 + n_q * hd * 2 + 2 * n_kv * hd * 4 + 4 * LANES * 4)
    return pl.pallas_call(
        functools.partial(_qk_post_body, n_q=n_q, n_kv=n_kv, rope=rope, forget=forget),
        grid=(mp // t,),
        in_specs=specs,
        out_specs=out_specs,
        out_shape=out_shape,
        compiler_params=_cparams(1, vmem),
        name="qk_post",
    )(*args)


def _attn_prompt_body(*refs, mode, seq, tq, batch):
    o_ref = refs[-1]
    b = pl.program_id(0)

    @pl.when(b < batch)
    def _():
        _attn_prompt_compute(*refs, mode=mode, seq=seq, tq=tq)

    @pl.when(b >= batch)
    def _():
        o_ref[...] = jnp.zeros(o_ref.shape, o_ref.dtype)


def _attn_prompt_compute(*refs, mode, seq, tq):
    if mode == "fox":
        q_ref, k_ref, v_ref, cq_ref, ck_ref, o_ref = refs
    else:
        q_ref, k_ref, v_ref, o_ref = refs
    hd = HEAD_DIM
    nblk = seq // tq
    kf = k_ref[...]
    kb = kf.astype(BF16)
    vb = v_ref[...].astype(BF16)
    row = lax.broadcasted_iota(I32, (tq, tq), 0)
    col = lax.broadcasted_iota(I32, (tq, tq), 1)
    causal = col <= row
    if mode == "moba":
        kmean = jnp.mean(kf.reshape(nblk, tq, hd), axis=1)
        kmean = jnp.concatenate([kmean, jnp.zeros((LANES - nblk, hd), F32)], axis=0)
        lane = lax.broadcasted_iota(I32, (tq, LANES), 1)
        eb = lax.broadcasted_iota(I32, (LANES, seq), 0)
        ec = lax.broadcasted_iota(I32, (LANES, seq), 1)
        expand = jnp.where(jnp.logical_and(ec >= eb * tq, ec < (eb + 1) * tq), 1.0, 0.0).astype(BF16)
    for i in range(nblk):
        n_past = i * tq
        n = n_past + tq
        for hh in range(GROUP):
            qh = q_ref[i * tq:(i + 1) * tq, hh * hd:(hh + 1) * hd]
            s = lax.dot_general(qh, kb[:n], (((1,), (1,)), ((), ())), preferred_element_type=F32) * ATTN_SCALE
            if mode == "fox":
                s = s + (cq_ref[i * tq:(i + 1) * tq, hh:hh + 1] - ck_ref[hh:hh + 1, :n])
            s_diag = jnp.where(causal, s[:, n_past:], NEG_INF)
            if i == 0:
                s = s_diag
            else:
                s_past = s[:, :n_past]
                if mode == "moba" and i > MOBA_TOPK:
                    gate = lax.dot_general(qh.astype(F32), kmean, (((1,), (1,)), ((), ())),
                                           preferred_element_type=F32, precision=lax.Precision.HIGHEST)
                    gm = jnp.where(lane < i, gate, NEG_INF)
                    rank = jnp.zeros((tq, LANES), I32)
                    for d in range(1, i):
                        lower = pltpu.roll(gm, d, 1)
                        higher = pltpu.roll(gm, LANES - d, 1)
                        rank = rank + (lower >= gm).astype(I32) + (higher > gm).astype(I32)
                    sel = jnp.logical_and(rank < MOBA_TOPK, lane < i)
                    picked = jnp.dot(jnp.where(sel, 1.0, 0.0).astype(BF16), expand[:, :n_past],
                                     preferred_element_type=F32)
                    s_past = jnp.where(picked > 0.5, s_past, NEG_INF)
                s = jnp.concatenate([s_past, s_diag], axis=1)
            m = jnp.max(s, axis=1, keepdims=True)
            p = jnp.exp(s - m)
            l = jnp.sum(p, axis=1, keepdims=True)
            o = jnp.dot(p.astype(BF16), vb[:n], preferred_element_type=F32) / l
            o_ref[i * tq:(i + 1) * tq, hh * hd:(hh + 1) * hd] = o.astype(o_ref.dtype)


def _attn_prompt(q, k, v, batch, seq, mode, cq=None, ck=None):
    mp = q.shape[0]
    hd = HEAD_DIM
    n_kv = k.shape[1] // hd
    tq = MOBA_BLOCK
    assert mp > batch * seq and mp - batch * seq <= seq
    src = lambda b, g: (jnp.minimum(b, batch - 1), g)
    src4 = lambda b, g: (jnp.minimum(b, batch - 1), g, 0, 0)
    in_specs = [pl.BlockSpec((seq, GROUP * hd), src),
                pl.BlockSpec((seq, hd), src),
                pl.BlockSpec((seq, hd), src)]
    args = [q, k, v]
    if mode == "fox":
        in_specs += [pl.BlockSpec((None, None, seq, GROUP), src4),
                     pl.BlockSpec((None, None, GROUP, seq), src4)]
        args += [cq, ck]
    vmem = 2 * (2 * seq * GROUP * hd * 2 + 2 * seq * hd * 4 + seq * LANES * 4 + 8 * seq * 4) + 20 * tq * seq * 4
    return pl.pallas_call(
        functools.partial(_attn_prompt_body, mode=mode, seq=seq, tq=tq, batch=batch),
        grid=(batch + 1, n_kv),
        in_specs=in_specs,
        out_specs=pl.BlockSpec((seq, GROUP * hd), lambda b, g: (b, g)),
        out_shape=jax.ShapeDtypeStruct((mp, q.shape[1]), BF16),
        compiler_params=_cparams(2, vmem),
        name=f"attn_prompt_{mode}",
    )(*args)


def _cumsum_body(x_ref, o_ref):
    rows, seq = x_ref.shape
    r = lax.broadcasted_iota(I32, (LANES, LANES), 0)
    c = lax.broadcasted_iota(I32, (LANES, LANES), 1)
    tri = jnp.where(r <= c, 1.0, 0.0).astype(BF16)
    ones = jnp.ones((LANES, LANES), BF16)
    carry = jnp.zeros((rows, LANES), F32)
    for ch in range(seq // LANES):
        parts = _split3(x_ref[:, ch * LANES:(ch + 1) * LANES])
        o_ref[:, ch * LANES:(ch + 1) * LANES] = carry + _dot3(parts, tri)
        carry = carry + _dot3(parts, ones)


def _cumsum_lanes(x):
    b, r, l = x.shape
    spec = pl.BlockSpec((None, r, l), lambda i: (i, 0, 0))
    return pl.pallas_call(
        _cumsum_body, grid=(b,), in_specs=[spec], out_specs=spec,
        out_shape=jax.ShapeDtypeStruct(x.shape, F32),
        compiler_params=_cparams(1, 4 * r * l * 4),
        name="cumsum_lanes",
    )(x)


def _kmean_body(pt_ref, *refs, pages, per_block, n_kv):
    page_refs, o_ref = refs[:pages], refs[pages]
    keys = page_refs[0].shape[0] // n_kv
    for j in range(pages // per_block):
        tot = jnp.sum(page_refs[j * per_block][...].reshape(keys, n_kv, HEAD_DIM), axis=0)
        for t in range(1, per_block):
            tot = tot + jnp.sum(page_refs[j * per_block + t][...].reshape(keys, n_kv, HEAD_DIM), axis=0)
        o_ref[j] = tot * (1.0 / (per_block * keys))


def _cache_block_means(cache_rows, page_table, page, n_kv):
    db, n_pages = page_table.shape
    hd = cache_rows.shape[1]
    per_block = MOBA_BLOCK // page
    pages = PAGES_PER_STEP
    assert n_pages % pages == 0 and pages % per_block == 0
    blocks = pages // per_block

    def pmap(j):
        return lambda b, c, pt: (pt[b, c * pages + j], 0)

    out = pl.pallas_call(
        functools.partial(_kmean_body, pages=pages, per_block=per_block, n_kv=n_kv),
        grid_spec=pltpu.PrefetchScalarGridSpec(
            num_scalar_prefetch=1,
            grid=(db, n_pages // pages),
            in_specs=[pl.BlockSpec((page * n_kv, hd), pmap(j)) for j in range(pages)],
            out_specs=pl.BlockSpec((None, None, blocks, n_kv, hd), lambda b, c, pt: (b, c, 0, 0, 0))),
        out_shape=jax.ShapeDtypeStruct((db, n_pages // pages, blocks, n_kv, hd), F32),
        compiler_params=_cparams(2, 3 * pages * page * n_kv * hd * 4),
        name="cache_block_means",
    )(page_table, *([cache_rows] * pages))
    return out.reshape(db, n_pages // per_block, n_kv, hd)


def _moba_select_body(q_ref, km_ref, o_ref, *, n_kv):
    q = q_ref[...]
    n_heads = q.shape[0]
    nb = km_ref.shape[1]
    rowg = lax.broadcasted_iota(I32, (n_heads, nb), 0) // GROUP
    gate = jnp.zeros((n_heads, nb), F32)
    for g in range(n_kv):
        gg = lax.dot_general(q, km_ref[g], (((1,), (1,)), ((), ())),
                             preferred_element_type=F32, precision=lax.Precision.HIGHEST)
        gate = jnp.where(rowg == g, gg, gate)
    lane = lax.broadcasted_iota(I32, (n_heads, nb), 1)
    olane = lax.broadcasted_iota(I32, (n_heads, LANES), 1)
    out = jnp.zeros((n_heads, LANES), I32)
    for j in range(MOBA_TOPK):
        m = jnp.max(gate, axis=1, keepdims=True)
        idx = jnp.min(jnp.where(gate == m, lane, nb), axis=1, keepdims=True)
        out = jnp.where(olane == j, idx, out)
        gate = jnp.where(lane == idx, NEG_INF, gate)
    o_ref[...] = out


def _moba_select(q_s, kmean):
    db, n_heads, hd = q_s.shape
    n_kv, nb = kmean.shape[1:3]
    out = pl.pallas_call(
        functools.partial(_moba_select_body, n_kv=n_kv),
        grid=(db,),
        in_specs=[pl.BlockSpec((None, n_heads, hd), lambda b: (b, 0, 0)),
                  pl.BlockSpec((None, n_kv, nb, hd), lambda b: (b, 0, 0, 0))],
        out_specs=pl.BlockSpec((None, n_heads, LANES), lambda b: (b, 0, 0)),
        out_shape=jax.ShapeDtypeStruct((db, n_heads, LANES), I32),
        compiler_params=_cparams(1, 4 * n_kv * nb * hd * 4),
        name="moba_select",
    )(q_s, kmean)
    return out[:, :, :MOBA_TOPK]


def _moba_sample_body(pt_ref, sel_ref, q_ref, kn_ref, vn_ref, *refs, n_pages, n_kv):
    k_refs, v_refs, o_ref = refs[:n_pages], refs[n_pages:2 * n_pages], refs[2 * n_pages]
    q = q_ref[...]
    q8 = jnp.broadcast_to(q, (8, HEAD_DIM)).astype(BF16)
    g = pl.program_id(1) // GROUP
    lane = lax.broadcasted_iota(I32, (1, k_refs[0].shape[0]), 1)
    mine = lane % n_kv == g
    scores = [jnp.where(mine, lax.dot_general(q8, kr[...].astype(BF16), (((1,), (1,)), ((), ())),
                                              preferred_element_type=F32)[0:1] * ATTN_SCALE, NEG_INF)
              for kr in k_refs]
    s_own = jnp.sum(q * kn_ref[...], axis=1, keepdims=True) * ATTN_SCALE
    m = s_own
    for s in scores:
        m = jnp.maximum(m, jnp.max(s, axis=1, keepdims=True))
    p_own = jnp.exp(s_own - m)
    l = p_own
    acc = p_own * vn_ref[...]
    for s, vr in zip(scores, v_refs):
        p = jnp.exp(s - m)
        l = l + jnp.sum(p, axis=1, keepdims=True)
        p8 = jnp.broadcast_to(p, (8, p.shape[1])).astype(BF16)
        acc = acc + jnp.dot(p8, vr[...].astype(BF16), preferred_element_type=F32)[0:1]
    o_ref[...] = acc / l


def _moba_sample_attend(q_s, k_new, v_new, ck_rows, cv_rows, page_table, sel, page, n_kv):
    db, n_heads, hd = q_s.shape
    per_block = MOBA_BLOCK // page
    n_pages = MOBA_TOPK * per_block

    def pmap(j):
        blk, t = j // per_block, j % per_block
        return lambda b, h, pt, sl: (pt[b, sl[b, h * MOBA_TOPK + blk] * per_block + t], 0)

    one = lambda b, h, pt, sl: (b, h, 0, 0)
    kv_one = lambda b, h, pt, sl: (b, h // GROUP, 0, 0)
    row = (None, None, 1, hd)
    page_spec = [pl.BlockSpec((page * n_kv, hd), pmap(j)) for j in range(n_pages)]
    out = pl.pallas_call(
        functools.partial(_moba_sample_body, n_pages=n_pages, n_kv=n_kv),
        grid_spec=pltpu.PrefetchScalarGridSpec(
            num_scalar_prefetch=2,
            grid=(db, n_heads),
            in_specs=[pl.BlockSpec(row, one), pl.BlockSpec(row, kv_one), pl.BlockSpec(row, kv_one)]
            + page_spec + page_spec,
            out_specs=pl.BlockSpec(row, one)),
        out_shape=jax.ShapeDtypeStruct((db, n_heads, 1, hd), F32),
        compiler_params=_cparams(2, 3 * 2 * n_pages * page * n_kv * hd * 4),
        name="moba_sample_attend",
    )(page_table, sel.reshape(db, n_heads * MOBA_TOPK), q_s.reshape(db, n_heads, 1, hd),
      k_new.reshape(db, n_kv, 1, hd), v_new.reshape(db, n_kv, 1, hd),
      *([ck_rows] * n_pages), *([cv_rows] * n_pages))
    return out.reshape(db, n_heads * hd)


def _page_gather_body(pt_ref, *refs, pages):
    o_ref = refs[pages]
    rows = refs[0].shape[0]
    for j in range(pages):
        o_ref[j * rows:(j + 1) * rows, :] = refs[j][...]


def _gather_logf_pages(cache_logf, page_table):
    db, n_tab = page_table.shape
    _, page, n_heads = cache_logf.shape
    pages = 2 * PAGES_PER_STEP
    assert n_tab % pages == 0

    def pmap(j):
        return lambda b, c, pt: (pt[b, c * pages + j], 0, 0)

    return pl.pallas_call(
        functools.partial(_page_gather_body, pages=pages),
        grid_spec=pltpu.PrefetchScalarGridSpec(
            num_scalar_prefetch=1,
            grid=(db, n_tab // pages),
            in_specs=[pl.BlockSpec((None, page, n_heads), pmap(j)) for j in range(pages)],
            out_specs=pl.BlockSpec((None, pages * page, n_heads), lambda b, c, pt: (b, c, 0))),
        out_shape=jax.ShapeDtypeStruct((db, n_tab * page, n_heads), F32),
        compiler_params=_cparams(2, 4 * pages * page * LANES * 4),
        name="gather_logf_pages",
    )(page_table, *([cache_logf] * pages))


def _suffix_body(x_ref, init_ref, o_ref, r_ref):
    c = pl.program_id(1)

    @pl.when(c == 0)
    def _():
        r_ref[...] = init_ref[...]

    rr = lax.broadcasted_iota(I32, (LANES, LANES), 0)
    cc = lax.broadcasted_iota(I32, (LANES, LANES), 1)
    later = jnp.where(rr > cc, 1.0, 0.0).astype(BF16)
    ones = jnp.ones((LANES, LANES), BF16)
    for j in range(x_ref.shape[1] // LANES - 1, -1, -1):
        parts = _split3(x_ref[:, j * LANES:(j + 1) * LANES])
        r = r_ref[...]
        o_ref[:, j * LANES:(j + 1) * LANES] = r + _dot3(parts, later)
        r_ref[...] = r + _dot3(parts, ones)


def _suffix_sum_lanes(x, init, chunk=2048):
    b, r, l = x.shape
    n_chunks = l // chunk
    assert l % chunk == 0
    rev = lambda i, c: (i, 0, n_chunks - 1 - c)
    return pl.pallas_call(
        _suffix_body,
        grid=(b, n_chunks),
        in_specs=[pl.BlockSpec((None, r, chunk), rev), pl.BlockSpec((None, r, LANES), lambda i, c: (i, 0, 0))],
        out_specs=pl.BlockSpec((None, r, chunk), rev),
        out_shape=jax.ShapeDtypeStruct(x.shape, F32),
        scratch_shapes=[pltpu.VMEM((r, LANES), F32)],
        compiler_params=_cparams(2, 8 * r * chunk * 4),
        name="suffix_sum_lanes",
    )(x, init)


def _fox_sample_body(pt_ref, q_ref, kn_ref, vn_ref, bias_ref, *refs, pages, n_kv):
    k_refs, v_refs = refs[:pages], refs[pages:2 * pages]
    o_ref, m_ref, l_ref, acc_ref = refs[2 * pages:]
    c = pl.program_id(1)
    n_heads = q_ref.shape[0]
    rows = k_refs[0].shape[0]

    @pl.when(c == 0)
    def _():
        m_ref[...] = jnp.full(m_ref.shape, NEG_INF, F32)
        l_ref[...] = jnp.zeros(l_ref.shape, F32)
        acc_ref[...] = jnp.zeros(acc_ref.shape, F32)

    q = q_ref[...]

    def absorb(s, vb):
        m_old = m_ref[...]
        m_new = jnp.maximum(m_old, jnp.max(s, axis=1, keepdims=True))
        alpha = jnp.exp(m_old - m_new)
        p = jnp.exp(s - m_new[:, 0:1])
        l_ref[...] = alpha * l_ref[...] + jnp.sum(p, axis=1, keepdims=True)
        acc_ref[...] = acc_ref[...] * alpha + jnp.dot(p.astype(BF16), vb, preferred_element_type=F32)
        m_ref[...] = m_new

    for j in range(pages):
        s = lax.dot_general(q, k_refs[j][...].astype(BF16), (((1,), (1,)), ((), ())),
                            preferred_element_type=F32) * ATTN_SCALE + bias_ref[:, j * rows:(j + 1) * rows]
        absorb(s, v_refs[j][...].astype(BF16))

    @pl.when(c == pl.num_programs(1) - 1)
    def _():
        s = lax.dot_general(q, kn_ref[...].astype(BF16), (((1,), (1,)), ((), ())),
                            preferred_element_type=F32) * ATTN_SCALE
        rowg = lax.broadcasted_iota(I32, s.shape, 0) // GROUP
        lane = lax.broadcasted_iota(I32, s.shape, 1)
        absorb(jnp.where(lane == rowg, s, NEG_INF), vn_ref[...].astype(BF16))
        o_ref[...] = acc_ref[...] / l_ref[...]


def _fox_sample_attend(q_s, k_new, v_new, bias, ck_rows, cv_rows, page_table, page, n_kv):
    db, n_heads, hd = q_s.shape
    n_tab = page_table.shape[1]
    pages = PAGES_PER_STEP
    n_chunks = n_tab // pages
    rows = page * n_kv
    assert n_tab % pages == 0 and k_new.shape[1] == LANES

    def pmap(j):
        return lambda b, c, pt: (pt[b, c * pages + j], 0)

    per_b = lambda b, c, pt: (b, 0, 0)
    page_spec = [pl.BlockSpec((rows, hd), pmap(j)) for j in range(pages)]
    out = pl.pallas_call(
        functools.partial(_fox_sample_body, pages=pages, n_kv=n_kv),
        grid_spec=pltpu.PrefetchScalarGridSpec(
            num_scalar_prefetch=1,
            grid=(db, n_chunks),
            in_specs=[pl.BlockSpec((None, n_heads, hd), per_b),
                      pl.BlockSpec((None, LANES, hd), per_b),
                      pl.BlockSpec((None, LANES, hd), per_b),
                      pl.BlockSpec((None, n_heads, pages * rows), lambda b, c, pt: (b, 0, c))]
            + page_spec + page_spec,
            out_specs=pl.BlockSpec((None, n_heads, hd), per_b),
            scratch_shapes=[pltpu.VMEM((n_heads, LANES), F32), pltpu.VMEM((n_heads, LANES), F32),
                            pltpu.VMEM((n_heads, hd), F32)]),
        out_shape=jax.ShapeDtypeStruct((db, n_heads, hd), F32),
        compiler_params=_cparams(2, 3 * 2 * pages * rows * hd * 4 + 4 * n_heads * pages * rows * 4),
        name="fox_sample_attend",
    )(page_table, q_s, k_new, v_new, bias, *([ck_rows] * pages), *([cv_rows] * pages))
    return out.reshape(db, n_heads * hd)


def _row_copy(src_hbm, row, dst, slot, sem):
    return pltpu.make_async_copy(src_hbm.at[pl.ds(row, 1)], dst.at[pl.ds(slot, 1)], sem)


def _gather_norm_body(tok_ref, used_ref, x_hbm, g_ref, o_ref, buf_ref, sem, *, rows, per_sb):
    i = pl.program_id(0)
    base = i * rows
    live = (i % per_sb) * rows < used_ref[i // per_sb]

    @pl.when(live)
    def _():
        def issue(r, carry):
            _row_copy(x_hbm, tok_ref[base + r], buf_ref, r, sem).start()
            return carry

        def wait(r, carry):
            _row_copy(x_hbm, tok_ref[base + r], buf_ref, r, sem).wait()
            return carry

        lax.fori_loop(0, rows, issue, 0)
        lax.fori_loop(0, rows, wait, 0)
        o_ref[...] = _rms(buf_ref[...], g_ref[...]).astype(o_ref.dtype)

    @pl.when(jnp.logical_not(live))
    def _():
        o_ref[...] = jnp.zeros(o_ref.shape, o_ref.dtype)


def _moe_gather_norm(x, g, slot_tok, sb_used_rows, sb_rows, rows=256):
    cap = slot_tok.shape[0]
    d = x.shape[1]
    assert cap % sb_rows == 0 and sb_rows % rows == 0
    return pl.pallas_call(
        functools.partial(_gather_norm_body, rows=rows, per_sb=sb_rows // rows),
        grid_spec=pltpu.PrefetchScalarGridSpec(
            num_scalar_prefetch=2,
            grid=(cap // rows,),
            in_specs=[pl.BlockSpec(memory_space=pl.ANY), pl.BlockSpec((1, d), lambda i, t, n: (0, 0))],
            out_specs=pl.BlockSpec((rows, d), lambda i, t, n: (i, 0)),
            scratch_shapes=[pltpu.VMEM((rows, d), F32), pltpu.SemaphoreType.DMA(())]),
        out_shape=jax.ShapeDtypeStruct((cap, d), BF16),
        compiler_params=_cparams(1, rows * d * 4 * 3 + 2 * rows * d * 2),
        name="moe_gather_norm",
    )(slot_tok, sb_used_rows, x, g.reshape(1, d))


def _combine_body(pos_ref, x_ref, y_hbm, op_ref, ot_ref, buf_ref, sem, *, rows, n_tok, head_blocks):
    i = pl.program_id(0)
    base = i * rows

    def issue(r, carry):
        for j in range(MOE_TOPK):
            _row_copy(y_hbm, pos_ref[(base + r) * MOE_TOPK + j], buf_ref.at[j], r, sem).start()
        return carry

    def wait(r, carry):
        for j in range(MOE_TOPK):
            _row_copy(y_hbm, pos_ref[(base + r) * MOE_TOPK + j], buf_ref.at[j], r, sem).wait()
        return carry

    lax.fori_loop(0, rows, issue, 0)
    lax.fori_loop(0, rows, wait, 0)
    tok = base + lax.broadcasted_iota(I32, (rows, 1), 0)
    y = buf_ref[0]
    for j in range(1, MOE_TOPK):
        y = y + buf_ref[j]
    out = x_ref[...] + jnp.where(tok < n_tok, y, 0.0)

    @pl.when(i < head_blocks)
    def _():
        op_ref[...] = out

    @pl.when(i >= head_blocks)
    def _():
        ot_ref[...] = out


def _moe_combine(x, y, pos, n_tok, head_rows, rows=128):
    mp, d = x.shape
    head_blocks = head_rows // rows
    assert head_rows % rows == 0 and mp == head_rows + rows
    return pl.pallas_call(
        functools.partial(_combine_body, rows=rows, n_tok=n_tok, head_blocks=head_blocks),
        grid_spec=pltpu.PrefetchScalarGridSpec(
            num_scalar_prefetch=1,
            grid=(head_blocks + 1,),
            in_specs=[pl.BlockSpec((rows, d), lambda i, p: (i, 0)), pl.BlockSpec(memory_space=pl.ANY)],
            out_specs=[pl.BlockSpec((rows, d), lambda i, p: (jnp.minimum(i, head_blocks - 1), 0)),
                       pl.BlockSpec((rows, d), lambda i, p: (0, 0))],
            scratch_shapes=[pltpu.VMEM((MOE_TOPK, rows, d), F32), pltpu.SemaphoreType.DMA(())]),
        out_shape=[jax.ShapeDtypeStruct((head_rows, d), F32), jax.ShapeDtypeStruct((rows, d), F32)],
        compiler_params=_cparams(1, (MOE_TOPK + 6) * rows * d * 4),
        name="moe_combine",
    )(pos, x, y)


def _moe_plan(top_idx, top_gate, n_experts, zero_row):
    n_tok = top_idx.shape[0]
    n_asg = n_tok * MOE_TOPK
    sbr, sub = MOE_SUPER_ROWS, MOE_SUPER_ROWS // MOE_SUB_BLOCKS
    n_sb = n_experts + n_asg // sbr
    cap = n_sb * sbr
    e_flat = top_idx.reshape(n_asg)
    onehot = (e_flat[:, None] == jnp.arange(n_experts, dtype=I32)[None, :]).astype(I32)
    csum = jnp.cumsum(onehot, axis=0)
    rank = jnp.sum(csum * onehot, axis=1) - 1
    counts = csum[-1]
    sb_per_e = (counts + sbr - 1) // sbr
    sb_end = jnp.cumsum(sb_per_e)
    sb_start = sb_end - sb_per_e
    dest = (jnp.sum((sb_start * sbr)[None, :] * onehot, axis=1) + rank).astype(I32)
    tok = jnp.arange(n_asg, dtype=I32) // MOE_TOPK
    slot_tok = jnp.full((cap,), zero_row, I32).at[dest].set(tok)
    slot_gate = jnp.zeros((cap,), F32).at[dest].set(top_gate.reshape(n_asg))
    s = jnp.arange(n_sb, dtype=I32)
    used = s < sb_end[-1]
    last = jnp.maximum(sb_end[-1] - 1, 0)
    src = jnp.where(used, s, last).astype(I32)
    expert = jnp.minimum(jnp.searchsorted(sb_end, src, side="right"), n_experts - 1).astype(I32)
    rows_in = jnp.clip(counts[expert] - (src - sb_start[expert]) * sbr, 0, sbr)
    half = sub // 2
    units = jnp.where(used, (rows_in + half - 1) // half, 0).astype(I32)
    return slot_tok, slot_gate, dest, src, expert, units


def kernel(x_prompt, x_sample, cache_k0, cache_v0, cache_k1, cache_v1, cache_logf1, page_table, attn_norm, ffn_norm, w_qkv0, q_norm0, k_norm0, w_o0, w_qkvf1, b_f1, q_norm1, k_norm1, w_o1, w_gate0, w_up0, w_down0, w_router1, w_gate1, w_up1, w_down1):
    batch, seq, d_model = x_prompt.shape
    db, dec_seq, _ = x_sample.shape
    assert dec_seq == 1 and db <= SAMPLE_ROWS
    hd = HEAD_DIM
    n_heads = d_model // hd
    n_kv = cache_k0.shape[2]
    assert n_heads == GROUP * n_kv and seq % MOBA_BLOCK == 0
    page = cache_k0.shape[1]
    n_tab = page_table.shape[1]
    past_len = n_tab * page
    assert past_len % MOBA_BLOCK == 0
    n_experts = w_router1.shape[1]
    p_rows = batch * seq
    n_tok = p_rows + db
    mp = p_rows + SAMPLE_ROWS
    nq, nkv = n_heads * hd, n_kv * hd
    page_table = page_table.astype(I32)

    x0 = jnp.concatenate([x_prompt.reshape(p_rows, d_model), x_sample.reshape(db, d_model),
                          jnp.zeros((mp - n_tok, d_model), F32)], axis=0)

    pos = jnp.concatenate([jnp.tile(jnp.arange(seq, dtype=I32), batch), jnp.full((db,), past_len, I32),
                           jnp.zeros((mp - n_tok,), I32)])
    half = hd // 2
    inv_freq = ROPE_THETA ** (-jnp.arange(half, dtype=F32) * 2.0 / hd)
    ang = pos.astype(F32)[:, None] * inv_freq[None, :]
    cos = jnp.concatenate([jnp.cos(ang), jnp.cos(ang)], axis=1)
    sin = jnp.concatenate([-jnp.sin(ang), jnp.sin(ang)], axis=1)

    def place_sample_rows(o, o_s):
        blk = jnp.concatenate([o_s.astype(o.dtype), jnp.zeros((SAMPLE_ROWS - db, o.shape[1]), o.dtype)], axis=0)
        return lax.dynamic_update_slice(o, blk, (p_rows, 0))

    h = _rmsnorm(x0, attn_norm[0])
    qkv = _matmul(h, w_qkv0, nq + 2 * nkv, 512)
    q0, k0, v0 = _qk_post(qkv, q_norm0, k_norm0, n_heads, n_kv, cos=cos, sin=sin)
    o = _attn_prompt(q0, k0, v0, batch, seq, "moba")
    ck0 = cache_k0.reshape(-1, hd)
    cv0 = cache_v0.reshape(-1, hd)
    q0_s = q0[p_rows:n_tok].astype(F32).reshape(db, n_heads, hd)
    kmean = _cache_block_means(ck0, page_table, page, n_kv).transpose(0, 2, 1, 3)
    sel = _moba_select(q0_s, kmean)
    o_s = _moba_sample_attend(q0_s, k0[p_rows:n_tok], v0[p_rows:n_tok], ck0, cv0, page_table, sel, page, n_kv)
    o = place_sample_rows(o, o_s)
    a0 = _matmul(o, w_o0, d_model, 512)

    x1, h = _add_rmsnorm(x0, a0, ffn_norm[0])
    dense_sb = 4
    assert mp % (dense_sb * MOE_SUB_BLOCKS * 16) == 0
    d_src = jnp.arange(dense_sb, dtype=I32)
    d_exp = jnp.zeros((dense_sb,), I32)
    d_units = jnp.full((dense_sb,), 2 * MOE_SUB_BLOCKS, I32)
    act = _ffn_gate_up(h, w_gate0[None], w_up0[None], d_src, d_exp, d_units, mp // dense_sb, MOE_SUB_BLOCKS, 256)
    f0 = _ffn_down(act, w_down0[None], jnp.ones((mp, 1), F32), d_src, d_exp, d_units,
                   mp // dense_sb, MOE_SUB_BLOCKS, 1024, 1024)

    x2, h = _add_rmsnorm(x1, f0, attn_norm[1])
    qkv = _matmul(h, w_qkvf1, nq + 2 * nkv, 512)
    w_f = jnp.pad(w_qkvf1[:, nq + 2 * nkv:], ((0, 0), (0, LANES - n_heads)))
    fcols = _matmul(h, w_f, LANES, LANES)
    b_f = jnp.pad(b_f1.reshape(1, n_heads), ((0, 0), (0, LANES - n_heads)))
    q1, k1, v1, lf_pad = _qk_post(qkv, q_norm1, k_norm1, n_heads, n_kv, fcols=fcols, b_f=b_f)
    logf = lf_pad[:, :n_heads]
    lf_p = logf[:p_rows].reshape(batch, seq, n_heads)
    c_t = _cumsum_lanes(lf_p.transpose(0, 2, 1))
    ck = c_t.reshape(batch, n_kv, GROUP, seq)
    cq = ck.transpose(0, 1, 3, 2)
    o = _attn_prompt(q1, k1, v1, batch, seq, "fox", cq=cq, ck=ck)
    ck1 = cache_k1.reshape(-1, hd)
    cv1 = cache_v1.reshape(-1, hd)
    q1_s = q1[p_rows:n_tok].reshape(db, n_heads, hd)
    lf_new = jnp.broadcast_to(logf[p_rows:n_tok].reshape(db, n_heads, 1), (db, n_heads, LANES))
    lf_t = _gather_logf_pages(cache_logf1, page_table).transpose(0, 2, 1)
    fbias = _suffix_sum_lanes(lf_t, lf_new)
    own_head = (jnp.arange(n_kv, dtype=I32)[None, :] == jnp.arange(n_heads, dtype=I32)[:, None] // GROUP)
    fbias = jnp.where(own_head[None, :, None, :], fbias[:, :, :, None], NEG_INF).reshape(db, n_heads, past_len * n_kv)
    pad_new = lambda t: jnp.pad(t[p_rows:n_tok].reshape(db, n_kv, hd), ((0, 0), (0, LANES - n_kv), (0, 0)))
    o_s = _fox_sample_attend(q1_s, pad_new(k1), pad_new(v1), fbias, ck1, cv1, page_table, page, n_kv)
    o = place_sample_rows(o, o_s)
    a1 = _matmul(o, w_o1, d_model, 512)

    x3, ridx, rgate = _add_rmsnorm_router(x2, a1, ffn_norm[1], w_router1)
    slot_tok, slot_gate, dest, sb_src, sb_expert, sb_units = _moe_plan(
        ridx[:n_tok, :MOE_TOPK], rgate[:n_tok, :MOE_TOPK], n_experts, n_tok)
    half_rows = MOE_SUPER_ROWS // MOE_SUB_BLOCKS // 2
    xs = _moe_gather_norm(x3, ffn_norm[1], slot_tok, sb_units * half_rows, MOE_SUPER_ROWS, rows=half_rows)
    act = _ffn_gate_up(xs, w_gate1, w_up1, sb_src, sb_expert, sb_units, MOE_SUPER_ROWS, MOE_SUB_BLOCKS, 256)
    ys = _ffn_down(act, w_down1, slot_gate[:, None], sb_src, sb_expert, sb_units,
                   MOE_SUPER_ROWS, MOE_SUB_BLOCKS, 1024, 1024)
    pos_pad = jnp.concatenate([dest, jnp.zeros(((mp - n_tok) * MOE_TOPK,), I32)])
    y_p, y_tail = _moe_combine(x3, ys, pos_pad, n_tok, p_rows)
    y_p = y_p.reshape(batch, seq, d_model)
    y_s = y_tail[:db].reshape(db, 1, d_model)

    def split(t, tail):
        return t[:p_rows].reshape((batch, seq) + tail), t[p_rows:n_tok].reshape((db, 1) + tail)

    k0_p, k0_s = split(k0, (n_kv, hd))
    v0_p, v0_s = split(v0, (n_kv, hd))
    k1_p, k1_s = split(k1, (n_kv, hd))
    v1_p, v1_s = split(v1, (n_kv, hd))
    lf_p4, lf_s = split(logf, (n_heads,))
    return (y_p, y_s, k0_p, v0_p, k1_p, v1_p, lf_p4, k0_s, v0_s, k1_s, v1_s, lf_s)
```

```python
import functools

import jax
import jax.numpy as jnp
from jax import lax
from jax.experimental import pallas as pl
from jax.experimental.pallas import tpu as pltpu

F32 = jnp.float32
BF16 = jnp.bfloat16
I32 = jnp.int32

HEAD_DIM = 128
GROUP = 4
MOBA_BLOCK = 256
MOBA_TOPK = 3
MOE_TOPK = 2
ROPE_THETA = 10000.0
EPS = 1e-6
ATTN_SCALE = HEAD_DIM ** -0.5

LANES = 128
V7X_VMEM_BYTES = 64 * 1024 * 1024
COMPILER_SCRATCH_BYTES = 6 * 1024 * 1024
NEG_INF = float("-inf")

SAMPLE_ROWS = 128
MOE_SUPER_ROWS = 2560
MOE_SUB_BLOCKS = 5
PAGES_PER_STEP = 8
DMA_LOOP_UNROLL = 8


def _cparams(n_axes, vmem_bytes):
    limit = min(int(vmem_bytes) + COMPILER_SCRATCH_BYTES, V7X_VMEM_BYTES - 2 * 1024 * 1024)
    return pltpu.CompilerParams(
        dimension_semantics=("arbitrary",) * n_axes, vmem_limit_bytes=limit)


def _rms(x, g):
    ms = jnp.mean(x * x, axis=-1, keepdims=True)
    return x * lax.rsqrt(ms + EPS) * g


def _split3(x):
    a1 = x.astype(BF16)
    r1 = x - a1.astype(F32)
    a2 = r1.astype(BF16)
    r2 = r1 - a2.astype(F32)
    return a1, a2, r2.astype(BF16)


def _dot3(parts, m):
    acc = jnp.dot(parts[0], m, preferred_element_type=F32)
    acc = acc + jnp.dot(parts[1], m, preferred_element_type=F32)
    return acc + jnp.dot(parts[2], m, preferred_element_type=F32)


def _rms_body(x_ref, g_ref, h_ref):
    h_ref[...] = _rms(x_ref[...], g_ref[...]).astype(h_ref.dtype)


def _add_rms_body(x_ref, d_ref, g_ref, xo_ref, h_ref):
    x = x_ref[...] + d_ref[...]
    xo_ref[...] = x
    h_ref[...] = _rms(x, g_ref[...]).astype(h_ref.dtype)


def _add_rms_router_body(x_ref, d_ref, g_ref, wr_ref, xo_ref, idx_ref, gate_ref, *, n_experts):
    x = x_ref[...] + d_ref[...]
    xo_ref[...] = x
    h = _rms(x, g_ref[...])
    logits = jnp.dot(h, wr_ref[...], preferred_element_type=F32, precision=lax.Precision.HIGHEST)
    lane = lax.broadcasted_iota(I32, logits.shape, 1)
    s = jnp.where(lane < n_experts, logits, NEG_INF)
    m1 = jnp.max(s, axis=1, keepdims=True)
    i1 = jnp.min(jnp.where(s == m1, lane, LANES), axis=1, keepdims=True)
    s2 = jnp.where(lane == i1, NEG_INF, s)
    m2 = jnp.max(s2, axis=1, keepdims=True)
    i2 = jnp.min(jnp.where(s2 == m2, lane, LANES), axis=1, keepdims=True)
    e = jnp.exp(m2 - m1)
    g1 = 1.0 / (1.0 + e)
    g2 = e / (1.0 + e)
    idx_ref[...] = jnp.where(lane == 0, i1, jnp.where(lane == 1, i2, 0))
    gate_ref[...] = jnp.where(lane == 0, g1, jnp.where(lane == 1, g2, 0.0))


def _row_tile(mp):
    for t in (320, 256, 128):
        if mp % t == 0:
            return t
    raise ValueError(f"unsupported row count {mp}")


def _rmsnorm(x, g):
    mp, d = x.shape
    t = _row_tile(mp)
    return pl.pallas_call(
        _rms_body,
        grid=(mp // t,),
        in_specs=[pl.BlockSpec((t, d), lambda i: (i, 0)), pl.BlockSpec((1, d), lambda i: (0, 0))],
        out_specs=pl.BlockSpec((t, d), lambda i: (i, 0)),
        out_shape=jax.ShapeDtypeStruct((mp, d), BF16),
        compiler_params=_cparams(1, 2 * t * d * 6),
        name="rmsnorm",
    )(x, g.reshape(1, d))


def _add_rmsnorm(x, delta, g):
    mp, d = x.shape
    t = _row_tile(mp)
    row = pl.BlockSpec((t, d), lambda i: (i, 0))
    return pl.pallas_call(
        _add_rms_body,
        grid=(mp // t,),
        in_specs=[row, row, pl.BlockSpec((1, d), lambda i: (0, 0))],
        out_specs=[row, row],
        out_shape=[jax.ShapeDtypeStruct((mp, d), F32), jax.ShapeDtypeStruct((mp, d), BF16)],
        compiler_params=_cparams(1, 2 * t * d * 14),
        name="add_rmsnorm",
    )(x, delta, g.reshape(1, d))


def _add_rmsnorm_router(x, delta, g, w_router):
    mp, d = x.shape
    n_experts = w_router.shape[1]
    t = _row_tile(mp)
    wr = jnp.pad(w_router, ((0, 0), (0, LANES - n_experts)))
    row = pl.BlockSpec((t, d), lambda i: (i, 0))
    small = pl.BlockSpec((t, LANES), lambda i: (i, 0))
    return pl.pallas_call(
        functools.partial(_add_rms_router_body, n_experts=n_experts),
        grid=(mp // t,),
        in_specs=[row, row, pl.BlockSpec((1, d), lambda i: (0, 0)), pl.BlockSpec((d, LANES), lambda i: (0, 0))],
        out_specs=[row, small, small],
        out_shape=[jax.ShapeDtypeStruct((mp, d), F32), jax.ShapeDtypeStruct((mp, LANES), I32),
                   jax.ShapeDtypeStruct((mp, LANES), F32)],
        compiler_params=_cparams(1, 2 * t * d * 12 + 2 * d * LANES * 4 + 8 * t * d),
        name="add_rmsnorm_router",
    )(x, delta, g.reshape(1, d), wr)


def _mm_body(x_ref, w_ref, o_ref, wbf_ref):
    wbf_ref[...] = w_ref[...].astype(BF16)
    o_ref[...] = jnp.dot(x_ref[...], wbf_ref[...], preferred_element_type=F32).astype(o_ref.dtype)


def _matmul(x, w, n_out, bn):
    mp, k = x.shape
    bm = mp // 8
    assert mp % bm == 0 and n_out % bn == 0
    vmem = 2 * (bm * k * 2 + k * bn * 4 + bm * bn * 4) + k * bn * 2
    return pl.pallas_call(
        _mm_body,
        grid=(mp // bm, n_out // bn),
        in_specs=[pl.BlockSpec((bm, k), lambda m, n: (m, 0)),
                  pl.BlockSpec((k, bn), lambda m, n: (0, n))],
        out_specs=pl.BlockSpec((bm, bn), lambda m, n: (m, n)),
        out_shape=jax.ShapeDtypeStruct((mp, n_out), F32),
        scratch_shapes=[pltpu.VMEM((k, bn), BF16)],
        compiler_params=_cparams(2, vmem),
        name="matmul",
    )(x, w)


def _sub_block_cases(units, j):
    n_full = units // 2
    odd = units % 2
    return j < n_full, jnp.logical_and(j == n_full, odd == 1), jnp.logical_and(j >= n_full + odd, units > 0)


def _gu_body(src_ref, ex_ref, un_ref, x_ref, wg_ref, wu_ref, o_ref, wb_ref, *, n_sub, sub, all_full):
    units = un_ref[pl.program_id(0)]
    half = sub // 2
    bn = o_ref.shape[1]

    def round_weights():
        wb_ref[:, :bn] = wg_ref[...].astype(BF16)
        wb_ref[:, bn:] = wu_ref[...].astype(BF16)

    def swiglu(rows):
        gu = jnp.dot(x_ref[rows, :], wb_ref[...], preferred_element_type=F32)
        g, u = gu[:, :bn], gu[:, bn:]
        o_ref[rows, :] = (g * jax.nn.sigmoid(g) * u).astype(o_ref.dtype)

    def clear(rows):
        o_ref[rows, :] = jnp.zeros((rows.stop - rows.start, o_ref.shape[1]), o_ref.dtype)

    if all_full:
        round_weights()
        for j in range(n_sub):
            swiglu(slice(j * sub, (j + 1) * sub))
        return

    pl.when(units > 0)(round_weights)
    for j in range(n_sub):
        full, first_half, unused = _sub_block_cases(units, j)

        @pl.when(full)
        def _():
            swiglu(slice(j * sub, (j + 1) * sub))

        @pl.when(first_half)
        def _():
            swiglu(slice(j * sub, j * sub + half))
            clear(slice(j * sub + half, (j + 1) * sub))

        @pl.when(unused)
        def _():
            clear(slice(j * sub, (j + 1) * sub))


def _ffn_gate_up(x, w_gate, w_up, sb_src, sb_expert, sb_nsub, sb_rows, n_sub, bn, all_full=False):
    rows, k = x.shape
    _, _, f = w_gate.shape
    n_sb = rows // sb_rows
    n_n = f // bn
    sub = sb_rows // n_sub
    assert rows % sb_rows == 0 and f % bn == 0 and sb_rows % n_sub == 0

    def col(s, n, ns):
        return jnp.where(ns[s] == 0, n_n - 1, n)

    def xmap(s, n, src, ex, ns):
        return (src[s], 0)

    def wmap(s, n, src, ex, ns):
        return (ex[s], 0, col(s, n, ns))

    def omap(s, n, src, ex, ns):
        return (src[s], col(s, n, ns))

    vmem = sb_rows * k * 2 + 2 * (2 * k * bn * 4 + sb_rows * bn * 2) + 2 * k * bn * 2 + 4 * sub * bn * 4
    return pl.pallas_call(
        functools.partial(_gu_body, n_sub=n_sub, sub=sub, all_full=all_full),
        grid_spec=pltpu.PrefetchScalarGridSpec(
            num_scalar_prefetch=3,
            grid=(n_sb, n_n),
            in_specs=[pl.BlockSpec((sb_rows, k), xmap, pipeline_mode=pl.Buffered(1)),
                      pl.BlockSpec((None, k, bn), wmap),
                      pl.BlockSpec((None, k, bn), wmap)],
            out_specs=pl.BlockSpec((sb_rows, bn), omap),
            scratch_shapes=[pltpu.VMEM((k, 2 * bn), BF16)]),
        out_shape=jax.ShapeDtypeStruct((rows, f), BF16),
        compiler_params=_cparams(2, vmem),
        name="ffn_gate_up",
    )(sb_src, sb_expert, sb_nsub, x, w_gate, w_up)


def _dn_body(src_ref, ex_ref, un_ref, a_ref, w_ref, s_ref, o_ref, wb_ref, *, n_sub, sub, all_full):
    units = un_ref[pl.program_id(0)]
    k = pl.program_id(2)

    def round_weights():
        wb_ref[...] = w_ref[...].astype(BF16)

    @pl.when(jnp.logical_and(units > 0, k == 0))
    def _():
        o_ref[...] = jnp.zeros(o_ref.shape, F32)

    def accumulate(rows):
        o_ref[rows, :] += jnp.dot(a_ref[rows, :], wb_ref[...], preferred_element_type=F32)

    if all_full:
        round_weights()
        for j in range(n_sub):
            accumulate(slice(j * sub, (j + 1) * sub))
    else:
        pl.when(units > 0)(round_weights)
        for j in range(n_sub):
            full, first_half, _ = _sub_block_cases(units, j)

            @pl.when(full)
            def _():
                accumulate(slice(j * sub, (j + 1) * sub))

            @pl.when(first_half)
            def _():
                accumulate(slice(j * sub, j * sub + sub // 2))

    @pl.when(jnp.logical_and(units > 0, k == pl.num_programs(2) - 1))
    def _():
        o_ref[...] = o_ref[...] * s_ref[...]


def _ffn_down(act, w_down, row_scale, sb_src, sb_expert, sb_nsub, sb_rows, n_sub, bn, bk, all_full=False):
    rows, f = act.shape
    _, _, d = w_down.shape
    n_sb = rows // sb_rows
    n_n, n_k = d // bn, f // bk
    sub = sb_rows // n_sub
    assert rows % sb_rows == 0 and d % bn == 0 and f % bk == 0 and sb_rows % n_sub == 0

    def tile(s, n, k, ns):
        unused = ns[s] == 0
        return jnp.where(unused, n_n - 1, n), jnp.where(unused, n_k - 1, k)

    def amap(s, n, k, src, ex, ns):
        n, k = tile(s, n, k, ns)
        return (src[s], k)

    def wmap(s, n, k, src, ex, ns):
        n, k = tile(s, n, k, ns)
        return (ex[s], k, n)

    def smap(s, n, k, src, ex, ns):
        return (src[s], 0)

    def omap(s, n, k, src, ex, ns):
        n, k = tile(s, n, k, ns)
        return (src[s], n)

    vmem = (2 * (sb_rows * bk * 2 + bk * bn * 4 + sb_rows * bn * 4 + sb_rows * LANES * 4) + bk * bn * 2
            + 2 * sub * bn * 4)
    return pl.pallas_call(
        functools.partial(_dn_body, n_sub=n_sub, sub=sub, all_full=all_full),
        grid_spec=pltpu.PrefetchScalarGridSpec(
            num_scalar_prefetch=3,
            grid=(n_sb, n_n, n_k),
            in_specs=[pl.BlockSpec((sb_rows, bk), amap),
                      pl.BlockSpec((None, bk, bn), wmap),
                      pl.BlockSpec((sb_rows, 1), smap)],
            out_specs=pl.BlockSpec((sb_rows, bn), omap),
            scratch_shapes=[pltpu.VMEM((bk, bn), BF16)]),
        out_shape=jax.ShapeDtypeStruct((rows, d), F32),
        compiler_params=_cparams(3, vmem),
        name="ffn_down",
    )(sb_src, sb_expert, sb_nsub, act, w_down, row_scale)


def _qk_post_body(*refs, n_q, n_kv, rope, forget):
    it = iter(refs)
    qkv_ref = next(it)
    cos_ref = sin_ref = f_ref = bf_ref = lf_ref = None
    if rope:
        cos_ref, sin_ref = next(it), next(it)
    gq_ref, gk_ref = next(it), next(it)
    if forget:
        f_ref, bf_ref = next(it), next(it)
    q_ref, k_ref, v_ref = next(it), next(it), next(it)
    if forget:
        lf_ref = next(it)
    hd = HEAD_DIM
    for h in range(n_q + n_kv):
        t = qkv_ref[:, h * hd:(h + 1) * hd]
        y = _rms(t, gq_ref[...] if h < n_q else gk_ref[...])
        if rope:
            y = y * cos_ref[...] + pltpu.roll(y, hd // 2, 1) * sin_ref[...]
        if h < n_q:
            q_ref[:, h * hd:(h + 1) * hd] = y.astype(q_ref.dtype)
        else:
            k_ref[:, (h - n_q) * hd:(h - n_q + 1) * hd] = y
    v_ref[...] = qkv_ref[:, (n_q + n_kv) * hd:(n_q + 2 * n_kv) * hd]
    if forget:
        z = f_ref[...] + bf_ref[...]
        lf_ref[...] = jnp.minimum(z, 0.0) - jnp.log(1.0 + jnp.exp(-jnp.abs(z)))


def _qk_post(qkv, gq, gk, n_q, n_kv, cos=None, sin=None, fcols=None, b_f=None):
    mp = qkv.shape[0]
    hd = HEAD_DIM
    t = _row_tile(mp)
    rope = cos is not None
    forget = fcols is not None
    row = lambda w: pl.BlockSpec((t, w), lambda i: (i, 0))
    const = lambda w: pl.BlockSpec((1, w), lambda i: (0, 0))
    args, specs = [qkv], [row(qkv.shape[1])]
    if rope:
        args += [cos, sin]
        specs += [row(hd), row(hd)]
    args += [gq.reshape(1, hd), gk.reshape(1, hd)]
    specs += [const(hd), const(hd)]
    if forget:
        args += [fcols, b_f]
        specs += [row(LANES), const(LANES)]
    out_shape = [jax.ShapeDtypeStruct((mp, n_q * hd), BF16), jax.ShapeDtypeStruct((mp, n_kv * hd), F32),
                 jax.ShapeDtypeStruct((mp, n_kv * hd), F32)]
    out_specs = [row(n_q * hd), row(n_kv * hd), row(n_kv * hd)]
    if forget:
        out_shape.append(jax.ShapeDtypeStruct((mp, LANES), F32))
        out_specs.append(row(LANES))
    vmem = 2 * t * (qkv.shape[1] * 4 + n_q * hd * 2 + 2 * n_kv * hd * 4 + 4 * LANES * 4)
    return pl.pallas_call(
        functools.partial(_qk_post_body, n_q=n_q, n_kv=n_kv, rope=rope, forget=forget),
        grid=(mp // t,),
        in_specs=specs,
        out_specs=out_specs,
        out_shape=out_shape,
        compiler_params=_cparams(1, vmem),
        name="qk_post",
    )(*args)


def _attn_prompt_body(*refs, mode, seq, tq, batch):
    o_ref = refs[-1]
    b = pl.program_id(0)

    @pl.when(b < batch)
    def _():
        _attn_prompt_compute(*refs, mode=mode, seq=seq, tq=tq)

    @pl.when(b >= batch)
    def _():
        o_ref[...] = jnp.zeros(o_ref.shape, o_ref.dtype)


def _attn_prompt_compute(*refs, mode, seq, tq):
    if mode == "fox":
        q_ref, k_ref, v_ref, cq_ref, ck_ref, o_ref = refs
    else:
        q_ref, k_ref, v_ref, o_ref = refs
    hd = HEAD_DIM
    nblk = seq // tq
    kf = k_ref[...]
    kb = kf.astype(BF16)
    vb = v_ref[...].astype(BF16)
    row = lax.broadcasted_iota(I32, (tq, tq), 0)
    col = lax.broadcasted_iota(I32, (tq, tq), 1)
    causal = col <= row
    if mode == "moba":
        kmean = jnp.mean(kf.reshape(nblk, tq, hd), axis=1)
        kmean = jnp.concatenate([kmean, jnp.zeros((LANES - nblk, hd), F32)], axis=0)
        lane = lax.broadcasted_iota(I32, (tq, LANES), 1)
        eb = lax.broadcasted_iota(I32, (LANES, seq), 0)
        ec = lax.broadcasted_iota(I32, (LANES, seq), 1)
        expand = jnp.where(jnp.logical_and(ec >= eb * tq, ec < (eb + 1) * tq), 1.0, 0.0).astype(BF16)
    for i in range(nblk):
        n_past = i * tq
        n = n_past + tq
        for hh in range(GROUP):
            qh = q_ref[i * tq:(i + 1) * tq, hh * hd:(hh + 1) * hd]
            s = lax.dot_general(qh, kb[:n], (((1,), (1,)), ((), ())), preferred_element_type=F32) * ATTN_SCALE
            if mode == "fox":
                s = s + (cq_ref[i * tq:(i + 1) * tq, hh:hh + 1] - ck_ref[hh:hh + 1, :n])
            s_diag = jnp.where(causal, s[:, n_past:], NEG_INF)
            if i == 0:
                s = s_diag
            else:
                s_past = s[:, :n_past]
                if mode == "moba" and i > MOBA_TOPK:
                    gate = lax.dot_general(qh.astype(F32), kmean, (((1,), (1,)), ((), ())),
                                           preferred_element_type=F32, precision=lax.Precision.HIGHEST)
                    gm = jnp.where(lane < i, gate, NEG_INF)
                    rank = jnp.zeros((tq, LANES), I32)
                    for d in range(1, i):
                        lower = pltpu.roll(gm, d, 1)
                        higher = pltpu.roll(gm, LANES - d, 1)
                        rank = rank + (lower >= gm).astype(I32) + (higher > gm).astype(I32)
                    sel = jnp.logical_and(rank < MOBA_TOPK, lane < i)
                    picked = jnp.dot(jnp.where(sel, 1.0, 0.0).astype(BF16), expand[:, :n_past],
                                     preferred_element_type=F32)
                    s_past = jnp.where(picked > 0.5, s_past, NEG_INF)
                s = jnp.concatenate([s_past, s_diag], axis=1)
            m = jnp.max(s, axis=1, keepdims=True)
            p = jnp.exp(s - m)
            l = jnp.sum(p, axis=1, keepdims=True)
            o = jnp.dot(p.astype(BF16), vb[:n], preferred_element_type=F32) / l
            o_ref[i * tq:(i + 1) * tq, hh * hd:(hh + 1) * hd] = o.astype(o_ref.dtype)


def _attn_prompt(q, k, v, batch, seq, mode, cq=None, ck=None):
    mp = q.shape[0]
    hd = HEAD_DIM
    n_kv = k.shape[1] // hd
    tq = MOBA_BLOCK
    assert mp > batch * seq and mp - batch * seq <= seq
    src = lambda b, g: (jnp.minimum(b, batch - 1), g)
    src4 = lambda b, g: (jnp.minimum(b, batch - 1), g, 0, 0)
    in_specs = [pl.BlockSpec((seq, GROUP * hd), src),
                pl.BlockSpec((seq, hd), src),
                pl.BlockSpec((seq, hd), src)]
    args = [q, k, v]
    if mode == "fox":
        in_specs += [pl.BlockSpec((None, None, seq, GROUP), src4),
                     pl.BlockSpec((None, None, GROUP, seq), src4)]
        args += [cq, ck]
    vmem = 2 * (2 * seq * GROUP * hd * 2 + 2 * seq * hd * 4 + seq * LANES * 4 + 8 * seq * 4) + 20 * tq * seq * 4
    return pl.pallas_call(
        functools.partial(_attn_prompt_body, mode=mode, seq=seq, tq=tq, batch=batch),
        grid=(batch + 1, n_kv),
        in_specs=in_specs,
        out_specs=pl.BlockSpec((seq, GROUP * hd), lambda b, g: (b, g)),
        out_shape=jax.ShapeDtypeStruct((mp, q.shape[1]), BF16),
        compiler_params=_cparams(2, vmem),
        name=f"attn_prompt_{mode}",
    )(*args)


def _cumsum_body(x_ref, o_ref):
    rows, seq = x_ref.shape
    r = lax.broadcasted_iota(I32, (LANES, LANES), 0)
    c = lax.broadcasted_iota(I32, (LANES, LANES), 1)
    tri = jnp.where(r <= c, 1.0, 0.0).astype(BF16)
    ones = jnp.ones((LANES, LANES), BF16)
    carry = jnp.zeros((rows, LANES), F32)
    for ch in range(seq // LANES):
        parts = _split3(x_ref[:, ch * LANES:(ch + 1) * LANES])
        o_ref[:, ch * LANES:(ch + 1) * LANES] = carry + _dot3(parts, tri)
        carry = carry + _dot3(parts, ones)


def _cumsum_lanes(x):
    b, r, l = x.shape
    spec = pl.BlockSpec((None, r, l), lambda i: (i, 0, 0))
    return pl.pallas_call(
        _cumsum_body, grid=(b,), in_specs=[spec], out_specs=spec,
        out_shape=jax.ShapeDtypeStruct(x.shape, F32),
        compiler_params=_cparams(1, 4 * r * l * 4),
        name="cumsum_lanes",
    )(x)


def _kmean_body(pt_ref, *refs, pages, per_block, n_kv):
    page_refs, o_ref = refs[:pages], refs[pages]
    keys = page_refs[0].shape[0] // n_kv
    for j in range(pages // per_block):
        tot = jnp.sum(page_refs[j * per_block][...].reshape(keys, n_kv, HEAD_DIM), axis=0)
        for t in range(1, per_block):
            tot = tot + jnp.sum(page_refs[j * per_block + t][...].reshape(keys, n_kv, HEAD_DIM), axis=0)
        o_ref[j] = tot * (1.0 / (per_block * keys))


def _cache_block_means(cache_rows, page_table, page, n_kv):
    db, n_pages = page_table.shape
    hd = cache_rows.shape[1]
    per_block = MOBA_BLOCK // page
    pages = PAGES_PER_STEP
    assert n_pages % pages == 0 and pages % per_block == 0
    blocks = pages // per_block

    def pmap(j):
        return lambda b, c, pt: (pt[b, c * pages + j], 0)

    out = pl.pallas_call(
        functools.partial(_kmean_body, pages=pages, per_block=per_block, n_kv=n_kv),
        grid_spec=pltpu.PrefetchScalarGridSpec(
            num_scalar_prefetch=1,
            grid=(db, n_pages // pages),
            in_specs=[pl.BlockSpec((page * n_kv, hd), pmap(j)) for j in range(pages)],
            out_specs=pl.BlockSpec((None, None, blocks, n_kv, hd), lambda b, c, pt: (b, c, 0, 0, 0))),
        out_shape=jax.ShapeDtypeStruct((db, n_pages // pages, blocks, n_kv, hd), F32),
        compiler_params=_cparams(2, 3 * pages * page * n_kv * hd * 4),
        name="cache_block_means",
    )(page_table, *([cache_rows] * pages))
    return out.reshape(db, n_pages // per_block, n_kv, hd)


def _moba_select_body(q_ref, km_ref, o_ref, *, n_kv):
    q = q_ref[...]
    n_heads = q.shape[0]
    nb = km_ref.shape[1]
    rowg = lax.broadcasted_iota(I32, (n_heads, nb), 0) // GROUP
    gate = jnp.zeros((n_heads, nb), F32)
    for g in range(n_kv):
        gg = lax.dot_general(q, km_ref[g], (((1,), (1,)), ((), ())),
                             preferred_element_type=F32, precision=lax.Precision.HIGHEST)
        gate = jnp.where(rowg == g, gg, gate)
    lane = lax.broadcasted_iota(I32, (n_heads, nb), 1)
    olane = lax.broadcasted_iota(I32, (n_heads, LANES), 1)
    out = jnp.zeros((n_heads, LANES), I32)
    for j in range(MOBA_TOPK):
        m = jnp.max(gate, axis=1, keepdims=True)
        idx = jnp.min(jnp.where(gate == m, lane, nb), axis=1, keepdims=True)
        out = jnp.where(olane == j, idx, out)
        gate = jnp.where(lane == idx, NEG_INF, gate)
    o_ref[...] = out


def _moba_select(q_s, kmean):
    db, n_heads, hd = q_s.shape
    n_kv, nb = kmean.shape[1:3]
    out = pl.pallas_call(
        functools.partial(_moba_select_body, n_kv=n_kv),
        grid=(db,),
        in_specs=[pl.BlockSpec((None, n_heads, hd), lambda b: (b, 0, 0)),
                  pl.BlockSpec((None, n_kv, nb, hd), lambda b: (b, 0, 0, 0))],
        out_specs=pl.BlockSpec((None, n_heads, LANES), lambda b: (b, 0, 0)),
        out_shape=jax.ShapeDtypeStruct((db, n_heads, LANES), I32),
        compiler_params=_cparams(1, 4 * n_kv * nb * hd * 4),
        name="moba_select",
    )(q_s, kmean)
    return out[:, :, :MOBA_TOPK]


def _moba_sample_body(pt_ref, sel_ref, q_ref, kn_ref, vn_ref, *refs, n_pages, n_kv):
    k_refs, v_refs, o_ref = refs[:n_pages], refs[n_pages:2 * n_pages], refs[2 * n_pages]
    q = q_ref[...]
    q8 = jnp.broadcast_to(q, (8, HEAD_DIM)).astype(BF16)
    g = pl.program_id(1) // GROUP
    lane = lax.broadcasted_iota(I32, (1, k_refs[0].shape[0]), 1)
    mine = lane % n_kv == g
    scores = [jnp.where(mine, lax.dot_general(q8, kr[...].astype(BF16), (((1,), (1,)), ((), ())),
                                              preferred_element_type=F32)[0:1] * ATTN_SCALE, NEG_INF)
              for kr in k_refs]
    s_own = jnp.sum(q * kn_ref[...], axis=1, keepdims=True) * ATTN_SCALE
    m = s_own
    for s in scores:
        m = jnp.maximum(m, jnp.max(s, axis=1, keepdims=True))
    p_own = jnp.exp(s_own - m)
    l = p_own
    acc = p_own * vn_ref[...]
    for s, vr in zip(scores, v_refs):
        p = jnp.exp(s - m)
        l = l + jnp.sum(p, axis=1, keepdims=True)
        p8 = jnp.broadcast_to(p, (8, p.shape[1])).astype(BF16)
        acc = acc + jnp.dot(p8, vr[...].astype(BF16), preferred_element_type=F32)[0:1]
    o_ref[...] = acc / l


def _moba_sample_attend(q_s, k_new, v_new, ck_rows, cv_rows, page_table, sel, page, n_kv):
    db, n_heads, hd = q_s.shape
    per_block = MOBA_BLOCK // page
    n_pages = MOBA_TOPK * per_block

    def pmap(j):
        blk, t = j // per_block, j % per_block
        return lambda b, h, pt, sl: (pt[b, sl[b, h * MOBA_TOPK + blk] * per_block + t], 0)

    one = lambda b, h, pt, sl: (b, h, 0, 0)
    kv_one = lambda b, h, pt, sl: (b, h // GROUP, 0, 0)
    row = (None, None, 1, hd)
    page_spec = [pl.BlockSpec((page * n_kv, hd), pmap(j)) for j in range(n_pages)]
    out = pl.pallas_call(
        functools.partial(_moba_sample_body, n_pages=n_pages, n_kv=n_kv),
        grid_spec=pltpu.PrefetchScalarGridSpec(
            num_scalar_prefetch=2,
            grid=(db, n_heads),
            in_specs=[pl.BlockSpec(row, one), pl.BlockSpec(row, kv_one), pl.BlockSpec(row, kv_one)]
            + page_spec + page_spec,
            out_specs=pl.BlockSpec(row, one)),
        out_shape=jax.ShapeDtypeStruct((db, n_heads, 1, hd), F32),
        compiler_params=_cparams(2, 3 * 2 * n_pages * page * n_kv * hd * 4),
        name="moba_sample_attend",
    )(page_table, sel.reshape(db, n_heads * MOBA_TOPK), q_s.reshape(db, n_heads, 1, hd),
      k_new.reshape(db, n_kv, 1, hd), v_new.reshape(db, n_kv, 1, hd),
      *([ck_rows] * n_pages), *([cv_rows] * n_pages))
    return out.reshape(db, n_heads * hd)


def _page_gather_body(pt_ref, *refs, pages):
    o_ref = refs[pages]
    rows = refs[0].shape[0]
    for j in range(pages):
        o_ref[j * rows:(j + 1) * rows, :] = refs[j][...]


def _gather_logf_pages(cache_logf, page_table):
    db, n_tab = page_table.shape
    _, page, n_heads = cache_logf.shape
    pages = 2 * PAGES_PER_STEP
    assert n_tab % pages == 0

    def pmap(j):
        return lambda b, c, pt: (pt[b, c * pages + j], 0, 0)

    return pl.pallas_call(
        functools.partial(_page_gather_body, pages=pages),
        grid_spec=pltpu.PrefetchScalarGridSpec(
            num_scalar_prefetch=1,
            grid=(db, n_tab // pages),
            in_specs=[pl.BlockSpec((None, page, n_heads), pmap(j)) for j in range(pages)],
            out_specs=pl.BlockSpec((None, pages * page, n_heads), lambda b, c, pt: (b, c, 0))),
        out_shape=jax.ShapeDtypeStruct((db, n_tab * page, n_heads), F32),
        compiler_params=_cparams(2, 4 * pages * page * LANES * 4),
        name="gather_logf_pages",
    )(page_table, *([cache_logf] * pages))


def _suffix_body(x_ref, init_ref, o_ref, r_ref):
    c = pl.program_id(1)

    @pl.when(c == 0)
    def _():
        r_ref[...] = init_ref[...]

    rr = lax.broadcasted_iota(I32, (LANES, LANES), 0)
    cc = lax.broadcasted_iota(I32, (LANES, LANES), 1)
    later = jnp.where(rr > cc, 1.0, 0.0).astype(BF16)
    ones = jnp.ones((LANES, LANES), BF16)
    for j in range(x_ref.shape[1] // LANES - 1, -1, -1):
        parts = _split3(x_ref[:, j * LANES:(j + 1) * LANES])
        r = r_ref[...]
        o_ref[:, j * LANES:(j + 1) * LANES] = r + _dot3(parts, later)
        r_ref[...] = r + _dot3(parts, ones)


def _suffix_sum_lanes(x, init, chunk=2048):
    b, r, l = x.shape
    n_chunks = l // chunk
    assert l % chunk == 0
    rev = lambda i, c: (i, 0, n_chunks - 1 - c)
    return pl.pallas_call(
        _suffix_body,
        grid=(b, n_chunks),
        in_specs=[pl.BlockSpec((None, r, chunk), rev), pl.BlockSpec((None, r, LANES), lambda i, c: (i, 0, 0))],
        out_specs=pl.BlockSpec((None, r, chunk), rev),
        out_shape=jax.ShapeDtypeStruct(x.shape, F32),
        scratch_shapes=[pltpu.VMEM((r, LANES), F32)],
        compiler_params=_cparams(2, 8 * r * chunk * 4),
        name="suffix_sum_lanes",
    )(x, init)


def _fox_sample_body(pt_ref, q_ref, kn_ref, vn_ref, bias_ref, *refs, pages, n_kv):
    k_refs, v_refs = refs[:pages], refs[pages:2 * pages]
    o_ref, m_ref, l_ref, acc_ref = refs[2 * pages:]
    c = pl.program_id(1)
    n_heads = q_ref.shape[0]
    rows = k_refs[0].shape[0]

    @pl.when(c == 0)
    def _():
        m_ref[...] = jnp.full(m_ref.shape, NEG_INF, F32)
        l_ref[...] = jnp.zeros(l_ref.shape, F32)
        acc_ref[...] = jnp.zeros(acc_ref.shape, F32)

    q = q_ref[...]

    def absorb(s, vb):
        m_old = m_ref[...]
        m_new = jnp.maximum(m_old, jnp.max(s, axis=1, keepdims=True))
        alpha = jnp.exp(m_old - m_new)
        p = jnp.exp(s - m_new[:, 0:1])
        l_ref[...] = alpha * l_ref[...] + jnp.sum(p, axis=1, keepdims=True)
        acc_ref[...] = acc_ref[...] * alpha + jnp.dot(p.astype(BF16), vb, preferred_element_type=F32)
        m_ref[...] = m_new

    for j in range(pages):
        s = lax.dot_general(q, k_refs[j][...].astype(BF16), (((1,), (1,)), ((), ())),
                            preferred_element_type=F32) * ATTN_SCALE + bias_ref[:, j * rows:(j + 1) * rows]
        absorb(s, v_refs[j][...].astype(BF16))

    @pl.when(c == pl.num_programs(1) - 1)
    def _():
        s = lax.dot_general(q, kn_ref[...].astype(BF16), (((1,), (1,)), ((), ())),
                            preferred_element_type=F32) * ATTN_SCALE
        rowg = lax.broadcasted_iota(I32, s.shape, 0) // GROUP
        lane = lax.broadcasted_iota(I32, s.shape, 1)
        absorb(jnp.where(lane == rowg, s, NEG_INF), vn_ref[...].astype(BF16))
        o_ref[...] = acc_ref[...] / l_ref[...]


def _fox_sample_attend(q_s, k_new, v_new, bias, ck_rows, cv_rows, page_table, page, n_kv):
    db, n_heads, hd = q_s.shape
    n_tab = page_table.shape[1]
    pages = PAGES_PER_STEP
    n_chunks = n_tab // pages
    rows = page * n_kv
    assert n_tab % pages == 0 and k_new.shape[1] == LANES

    def pmap(j):
        return lambda b, c, pt: (pt[b, c * pages + j], 0)

    per_b = lambda b, c, pt: (b, 0, 0)
    page_spec = [pl.BlockSpec((rows, hd), pmap(j)) for j in range(pages)]
    out = pl.pallas_call(
        functools.partial(_fox_sample_body, pages=pages, n_kv=n_kv),
        grid_spec=pltpu.PrefetchScalarGridSpec(
            num_scalar_prefetch=1,
            grid=(db, n_chunks),
            in_specs=[pl.BlockSpec((None, n_heads, hd), per_b),
                      pl.BlockSpec((None, LANES, hd), per_b),
                      pl.BlockSpec((None, LANES, hd), per_b),
                      pl.BlockSpec((None, n_heads, pages * rows), lambda b, c, pt: (b, 0, c))]
            + page_spec + page_spec,
            out_specs=pl.BlockSpec((None, n_heads, hd), per_b),
            scratch_shapes=[pltpu.VMEM((n_heads, LANES), F32), pltpu.VMEM((n_heads, LANES), F32),
                            pltpu.VMEM((n_heads, hd), F32)]),
        out_shape=jax.ShapeDtypeStruct((db, n_heads, hd), F32),
        compiler_params=_cparams(2, 3 * 2 * pages * rows * hd * 4 + 4 * n_heads * pages * rows * 4),
        name="fox_sample_attend",
    )(page_table, q_s, k_new, v_new, bias, *([ck_rows] * pages), *([cv_rows] * pages))
    return out.reshape(db, n_heads * hd)


def _row_copy(src_hbm, row, dst, slot, sem):
    return pltpu.make_async_copy(src_hbm.at[pl.ds(row, 1)], dst.at[pl.ds(slot, 1)], sem)


def _gather_norm_body(tok_ref, used_ref, x_hbm, g_ref, o_ref, buf_ref, sem, *, rows, per_sb):
    i = pl.program_id(0)
    base = i * rows
    live = (i % per_sb) * rows < used_ref[i // per_sb]

    @pl.when(live)
    def _():
        def issue(r, carry):
            _row_copy(x_hbm, tok_ref[base + r], buf_ref, r, sem).start()
            return carry

        def wait(r, carry):
            _row_copy(x_hbm, tok_ref[base + r], buf_ref, r, sem).wait()
            return carry

        lax.fori_loop(0, rows, issue, 0, unroll=DMA_LOOP_UNROLL)
        lax.fori_loop(0, rows, wait, 0, unroll=DMA_LOOP_UNROLL)
        o_ref[...] = _rms(buf_ref[...], g_ref[...]).astype(o_ref.dtype)

    @pl.when(jnp.logical_not(live))
    def _():
        o_ref[...] = jnp.zeros(o_ref.shape, o_ref.dtype)


def _moe_gather_norm(x, g, slot_tok, sb_used_rows, sb_rows, rows=256):
    cap = slot_tok.shape[0]
    d = x.shape[1]
    assert cap % sb_rows == 0 and sb_rows % rows == 0
    return pl.pallas_call(
        functools.partial(_gather_norm_body, rows=rows, per_sb=sb_rows // rows),
        grid_spec=pltpu.PrefetchScalarGridSpec(
            num_scalar_prefetch=2,
            grid=(cap // rows,),
            in_specs=[pl.BlockSpec(memory_space=pl.ANY), pl.BlockSpec((1, d), lambda i, t, n: (0, 0))],
            out_specs=pl.BlockSpec((rows, d), lambda i, t, n: (i, 0)),
            scratch_shapes=[pltpu.VMEM((rows, d), F32), pltpu.SemaphoreType.DMA(())]),
        out_shape=jax.ShapeDtypeStruct((cap, d), BF16),
        compiler_params=_cparams(1, rows * d * 4 * 3 + 2 * rows * d * 2),
        name="moe_gather_norm",
    )(slot_tok, sb_used_rows, x, g.reshape(1, d))


def _combine_body(pos_ref, x_ref, y_hbm, op_ref, ot_ref, buf_ref, sem, *, rows, n_tok, head_blocks):
    i = pl.program_id(0)
    base = i * rows

    def issue(r, carry):
        for j in range(MOE_TOPK):
            _row_copy(y_hbm, pos_ref[(base + r) * MOE_TOPK + j], buf_ref.at[j], r, sem).start()
        return carry

    def wait(r, carry):
        for j in range(MOE_TOPK):
            _row_copy(y_hbm, pos_ref[(base + r) * MOE_TOPK + j], buf_ref.at[j], r, sem).wait()
        return carry

    lax.fori_loop(0, rows, issue, 0, unroll=DMA_LOOP_UNROLL)
    lax.fori_loop(0, rows, wait, 0, unroll=DMA_LOOP_UNROLL)
    tok = base + lax.broadcasted_iota(I32, (rows, 1), 0)
    y = buf_ref[0]
    for j in range(1, MOE_TOPK):
        y = y + buf_ref[j]
    out = x_ref[...] + jnp.where(tok < n_tok, y, 0.0)

    @pl.when(i < head_blocks)
    def _():
        op_ref[...] = out

    @pl.when(i >= head_blocks)
    def _():
        ot_ref[...] = out


def _moe_combine(x, y, pos, n_tok, head_rows, rows=128):
    mp, d = x.shape
    head_blocks = head_rows // rows
    assert head_rows % rows == 0 and mp == head_rows + rows
    return pl.pallas_call(
        functools.partial(_combine_body, rows=rows, n_tok=n_tok, head_blocks=head_blocks),
        grid_spec=pltpu.PrefetchScalarGridSpec(
            num_scalar_prefetch=1,
            grid=(head_blocks + 1,),
            in_specs=[pl.BlockSpec((rows, d), lambda i, p: (i, 0)), pl.BlockSpec(memory_space=pl.ANY)],
            out_specs=[pl.BlockSpec((rows, d), lambda i, p: (jnp.minimum(i, head_blocks - 1), 0)),
                       pl.BlockSpec((rows, d), lambda i, p: (0, 0))],
            scratch_shapes=[pltpu.VMEM((MOE_TOPK, rows, d), F32), pltpu.SemaphoreType.DMA(())]),
        out_shape=[jax.ShapeDtypeStruct((head_rows, d), F32), jax.ShapeDtypeStruct((rows, d), F32)],
        compiler_params=_cparams(1, (MOE_TOPK + 6) * rows * d * 4),
        name="moe_combine",
    )(pos, x, y)


def _moe_plan(top_idx, top_gate, n_experts, zero_row):
    n_tok = top_idx.shape[0]
    n_asg = n_tok * MOE_TOPK
    sbr, sub = MOE_SUPER_ROWS, MOE_SUPER_ROWS // MOE_SUB_BLOCKS
    n_sb = n_experts + n_asg // sbr
    cap = n_sb * sbr
    e_flat = top_idx.reshape(n_asg)
    onehot = (e_flat[:, None] == jnp.arange(n_experts, dtype=I32)[None, :]).astype(I32)
    csum = jnp.cumsum(onehot, axis=0)
    rank = jnp.sum(csum * onehot, axis=1) - 1
    counts = csum[-1]
    sb_per_e = (counts + sbr - 1) // sbr
    sb_end = jnp.cumsum(sb_per_e)
    sb_start = sb_end - sb_per_e
    dest = (jnp.sum((sb_start * sbr)[None, :] * onehot, axis=1) + rank).astype(I32)
    tok = jnp.arange(n_asg, dtype=I32) // MOE_TOPK
    slot_tok = jnp.full((cap,), zero_row, I32).at[dest].set(tok)
    slot_gate = jnp.zeros((cap,), F32).at[dest].set(top_gate.reshape(n_asg))
    s = jnp.arange(n_sb, dtype=I32)
    used = s < sb_end[-1]
    last = jnp.maximum(sb_end[-1] - 1, 0)
    src = jnp.where(used, s, last).astype(I32)
    expert = jnp.minimum(jnp.searchsorted(sb_end, src, side="right"), n_experts - 1).astype(I32)
    rows_in = jnp.clip(counts[expert] - (src - sb_start[expert]) * sbr, 0, sbr)
    half = sub // 2
    units = jnp.where(used, (rows_in + half - 1) // half, 0).astype(I32)
    return slot_tok, slot_gate, dest, src, expert, units


def kernel(x_prompt, x_sample, cache_k0, cache_v0, cache_k1, cache_v1, cache_logf1, page_table, attn_norm, ffn_norm, w_qkv0, q_norm0, k_norm0, w_o0, w_qkvf1, b_f1, q_norm1, k_norm1, w_o1, w_gate0, w_up0, w_down0, w_router1, w_gate1, w_up1, w_down1):
    batch, seq, d_model = x_prompt.shape
    db, dec_seq, _ = x_sample.shape
    assert dec_seq == 1 and db <= SAMPLE_ROWS
    hd = HEAD_DIM
    n_heads = d_model // hd
    n_kv = cache_k0.shape[2]
    assert n_heads == GROUP * n_kv and seq % MOBA_BLOCK == 0
    page = cache_k0.shape[1]
    n_tab = page_table.shape[1]
    past_len = n_tab * page
    assert past_len % MOBA_BLOCK == 0
    n_experts = w_router1.shape[1]
    p_rows = batch * seq
    n_tok = p_rows + db
    mp = p_rows + SAMPLE_ROWS
    nq, nkv = n_heads * hd, n_kv * hd
    page_table = page_table.astype(I32)

    x0 = jnp.concatenate([x_prompt.reshape(p_rows, d_model), x_sample.reshape(db, d_model),
                          jnp.zeros((mp - n_tok, d_model), F32)], axis=0)

    pos = jnp.concatenate([jnp.tile(jnp.arange(seq, dtype=I32), batch), jnp.full((db,), past_len, I32),
                           jnp.zeros((mp - n_tok,), I32)])
    half = hd // 2
    inv_freq = ROPE_THETA ** (-jnp.arange(half, dtype=F32) * 2.0 / hd)
    ang = pos.astype(F32)[:, None] * inv_freq[None, :]
    cos = jnp.concatenate([jnp.cos(ang), jnp.cos(ang)], axis=1)
    sin = jnp.concatenate([-jnp.sin(ang), jnp.sin(ang)], axis=1)

    def place_sample_rows(o, o_s):
        blk = jnp.concatenate([o_s.astype(o.dtype), jnp.zeros((SAMPLE_ROWS - db, o.shape[1]), o.dtype)], axis=0)
        return lax.dynamic_update_slice(o, blk, (p_rows, 0))

    h = _rmsnorm(x0, attn_norm[0])
    qkv = _matmul(h, w_qkv0, nq + 2 * nkv, 512)
    q0, k0, v0 = _qk_post(qkv, q_norm0, k_norm0, n_heads, n_kv, cos=cos, sin=sin)
    o = _attn_prompt(q0, k0, v0, batch, seq, "moba")
    ck0 = cache_k0.reshape(-1, hd)
    cv0 = cache_v0.reshape(-1, hd)
    q0_s = q0[p_rows:n_tok].astype(F32).reshape(db, n_heads, hd)
    kmean = _cache_block_means(ck0, page_table, page, n_kv).transpose(0, 2, 1, 3)
    sel = _moba_select(q0_s, kmean)
    o_s = _moba_sample_attend(q0_s, k0[p_rows:n_tok], v0[p_rows:n_tok], ck0, cv0, page_table, sel, page, n_kv)
    o = place_sample_rows(o, o_s)
    a0 = _matmul(o, w_o0, d_model, 512)

    x1, h = _add_rmsnorm(x0, a0, ffn_norm[0])
    dense_sb = 4
    assert mp % (dense_sb * MOE_SUB_BLOCKS * 16) == 0
    d_src = jnp.arange(dense_sb, dtype=I32)
    d_exp = jnp.zeros((dense_sb,), I32)
    d_units = jnp.full((dense_sb,), 2 * MOE_SUB_BLOCKS, I32)
    act = _ffn_gate_up(h, w_gate0[None], w_up0[None], d_src, d_exp, d_units, mp // dense_sb, MOE_SUB_BLOCKS, 256,
                       all_full=True)
    f0 = _ffn_down(act, w_down0[None], jnp.ones((mp, 1), F32), d_src, d_exp, d_units,
                   mp // dense_sb, MOE_SUB_BLOCKS, 1024, 1024, all_full=True)

    x2, h = _add_rmsnorm(x1, f0, attn_norm[1])
    qkv = _matmul(h, w_qkvf1, nq + 2 * nkv, 512)
    w_f = jnp.pad(w_qkvf1[:, nq + 2 * nkv:], ((0, 0), (0, LANES - n_heads)))
    fcols = _matmul(h, w_f, LANES, LANES)
    b_f = jnp.pad(b_f1.reshape(1, n_heads), ((0, 0), (0, LANES - n_heads)))
    q1, k1, v1, lf_pad = _qk_post(qkv, q_norm1, k_norm1, n_heads, n_kv, fcols=fcols, b_f=b_f)
    logf = lf_pad[:, :n_heads]
    lf_p = logf[:p_rows].reshape(batch, seq, n_heads)
    c_t = _cumsum_lanes(lf_p.transpose(0, 2, 1))
    ck = c_t.reshape(batch, n_kv, GROUP, seq)
    cq = ck.transpose(0, 1, 3, 2)
    o = _attn_prompt(q1, k1, v1, batch, seq, "fox", cq=cq, ck=ck)
    ck1 = cache_k1.reshape(-1, hd)
    cv1 = cache_v1.reshape(-1, hd)
    q1_s = q1[p_rows:n_tok].reshape(db, n_heads, hd)
    lf_new = jnp.broadcast_to(logf[p_rows:n_tok].reshape(db, n_heads, 1), (db, n_heads, LANES))
    lf_t = _gather_logf_pages(cache_logf1, page_table).transpose(0, 2, 1)
    fbias = _suffix_sum_lanes(lf_t, lf_new)
    own_head = (jnp.arange(n_kv, dtype=I32)[None, :] == jnp.arange(n_heads, dtype=I32)[:, None] // GROUP)
    fbias = jnp.where(own_head[None, :, None, :], fbias[:, :, :, None], NEG_INF).reshape(db, n_heads, past_len * n_kv)
    pad_new = lambda t: jnp.pad(t[p_rows:n_tok].reshape(db, n_kv, hd), ((0, 0), (0, LANES - n_kv), (0, 0)))
    o_s = _fox_sample_attend(q1_s, pad_new(k1), pad_new(v1), fbias, ck1, cv1, page_table, page, n_kv)
    o = place_sample_rows(o, o_s)
    a1 = _matmul(o, w_o1, d_model, 512)

    x3, ridx, rgate = _add_rmsnorm_router(x2, a1, ffn_norm[1], w_router1)
    slot_tok, slot_gate, dest, sb_src, sb_expert, sb_units = _moe_plan(
        ridx[:n_tok, :MOE_TOPK], rgate[:n_tok, :MOE_TOPK], n_experts, n_tok)
    half_rows = MOE_SUPER_ROWS // MOE_SUB_BLOCKS // 2
    xs = _moe_gather_norm(x3, ffn_norm[1], slot_tok, sb_units * half_rows, MOE_SUPER_ROWS, rows=half_rows)
    act = _ffn_gate_up(xs, w_gate1, w_up1, sb_src, sb_expert, sb_units, MOE_SUPER_ROWS, MOE_SUB_BLOCKS, 256)
    ys = _ffn_down(act, w_down1, slot_gate[:, None], sb_src, sb_expert, sb_units,
                   MOE_SUPER_ROWS, MOE_SUB_BLOCKS, 1024, 1024)
    pos_pad = jnp.concatenate([dest, jnp.zeros(((mp - n_tok) * MOE_TOPK,), I32)])
    y_p, y_tail = _moe_combine(x3, ys, pos_pad, n_tok, p_rows)
    y_p = y_p.reshape(batch, seq, d_model)
    y_s = y_tail[:db].reshape(db, 1, d_model)

    def split(t, tail):
        return t[:p_rows].reshape((batch, seq) + tail), t[p_rows:n_tok].reshape((db, 1) + tail)

    k0_p, k0_s = split(k0, (n_kv, hd))
    v0_p, v0_s = split(v0, (n_kv, hd))
    k1_p, k1_s = split(k1, (n_kv, hd))
    v1_p, v1_s = split(v1, (n_kv, hd))
    lf_p4, lf_s = split(logf, (n_heads,))
    return (y_p, y_s, k0_p, v0_p, k1_p, v1_p, lf_p4, k0_s, v0_s, k1_s, v1_s, lf_s)
```
